```python
import math
import jax, jax.numpy as jnp
from jax import lax
import numpy as np

D_MODEL = 2048
BATCH = 2
SEQ = 4096
DEPTH = 2
DEC_BATCH = 8
DEC_SEQ = 4096
PAST_LEN = 128

N_MIXERS = 2
N_HYENA_LAYERS = (DEPTH + N_MIXERS - 1) // N_MIXERS
N_ATTN_LAYERS = DEPTH // N_MIXERS

HY_WIDTH = D_MODEL
HY_ORDER = 2
HY_DIRS = 2
SHORT_CONV = 3
POS_EMB_DIM = 33
POS_BANDS = (POS_EMB_DIM - 1) // 2
FILTER_HIDDEN = 64
DECAY_TARGET = 1e-2
FAST_DECAY_PCT = 0.3
SLOW_DECAY_PCT = 1.5
MIN_DECAY = math.log(DECAY_TARGET) / SLOW_DECAY_PCT
MAX_DECAY = math.log(DECAY_TARGET) / FAST_DECAY_PCT

N_HEADS = 16
N_KV_HEADS = 4
HEAD_DIM = 128
GROUP = N_HEADS // N_KV_HEADS
ATTN_WIDTH = N_HEADS * HEAD_DIM
KV_WIDTH = N_KV_HEADS * HEAD_DIM
WINDOW = 128
BLOCK = 128

LN_EPS = 1e-5
DEEPNORM_ALPHA = (2 * DEPTH) ** 0.25
DEEPNORM_BETA = (8 * DEPTH) ** -0.25

kernel_name = "hyena_swa_alibi_deepnorm_encoder"

F32 = jnp.float32


def layer_norm(x, g, b):
    xf = x.astype(F32)
    mu = jnp.mean(xf, -1, keepdims=True)
    var = jnp.mean(jnp.square(xf - mu), -1, keepdims=True)
    y = (xf - mu) * lax.rsqrt(var + LN_EPS) * g.astype(F32) + b.astype(F32)
    return y.astype(x.dtype)


def centred_short_conv(u, w, b):
    L = u.shape[1]
    pad = SHORT_CONV // 2
    up = jnp.pad(u, ((0, 0), (pad, pad), (0, 0)))
    out = b
    for j in range(SHORT_CONV):
        out = out + up[:, j:j + L] * w[j]
    return out


def hyena_filters(L, w_f1, b_f1, fr1, w_f2, b_f2, fr2, w_f3, b_f3, fr3, w_f4):
    t_norm = jnp.linspace(0.0, 1.0, L, dtype=F32)
    w = 2.0 * math.pi * jnp.arange(L, dtype=F32) / L
    f = jnp.linspace(1e-4, POS_BANDS - 1, POS_BANDS, dtype=F32)
    ang = w[:, None] * f[None, :]
    feat = jnp.concatenate([t_norm[:, None], jnp.cos(ang), -jnp.sin(ang)], axis=-1)
    h = jnp.sin(fr1.astype(F32) * (feat @ w_f1.astype(F32) + b_f1.astype(F32)))
    h = jnp.sin(fr2.astype(F32) * (h @ w_f2.astype(F32) + b_f2.astype(F32)))
    h = jnp.sin(fr3.astype(F32) * (h @ w_f3.astype(F32) + b_f3.astype(F32)))
    h = (h @ w_f4.astype(F32)).reshape(L, HY_ORDER, HY_DIRS, HY_WIDTH)
    deltas = jnp.abs(jnp.linspace(MIN_DECAY, MAX_DECAY, HY_WIDTH, dtype=F32))
    decay = jnp.exp(-t_norm[:, None] * deltas[None, :])
    h = h * decay[:, None, None, :]
    k = jnp.concatenate([h[:, :, 0],
                         jnp.zeros((1, HY_ORDER, HY_WIDTH), F32),
                         h[:0:-1, :, 1]], axis=0)
    k = k * lax.rsqrt(jnp.sum(jnp.square(k), axis=0, keepdims=True) + 1e-12)
    return jnp.fft.rfft(k, axis=0)


def long_conv(z, kf):
    L = z.shape[1]
    Z = jnp.fft.rfft(z, n=2 * L, axis=1)
    return jnp.fft.irfft(Z * kf[None], n=2 * L, axis=1)[:, :L]


def hyena_mixer(x, w_in, b_in, w_sc, b_sc, w_f1, b_f1, fr1, w_f2, b_f2, fr2,
                w_f3, b_f3, fr3, w_f4, h_bias, w_out, b_out):
    L = x.shape[1]
    E = HY_WIDTH
    proj = x @ w_in + b_in
    u = centred_short_conv(proj[..., :3 * E], w_sc, b_sc)
    g = proj[..., 3 * E:]
    v, x1, x2 = jnp.split(u, 3, axis=-1)
    kf = hyena_filters(L, w_f1, b_f1, fr1, w_f2, b_f2, fr2, w_f3, b_f3, fr3, w_f4)
    z = v.astype(F32)
    for n, gate in enumerate((x1, x2)):
        z = gate.astype(F32) * (long_conv(z, kf[:, n]) + h_bias[n].astype(F32) * z)
    y = z.astype(x.dtype) * jax.nn.silu(g)
    return y @ w_out + b_out


def alibi_slopes():
    return jnp.exp2(-8.0 * jnp.arange(1, N_HEADS + 1, dtype=F32) / N_HEADS)


def banded_attention(q, k, v, sink):
    B, L = q.shape[0], q.shape[1]
    nb = L // BLOCK
    qb = q.reshape(B, nb, BLOCK, N_KV_HEADS, GROUP, HEAD_DIM)

    def windows(t):
        tp = jnp.pad(t, ((0, 0), (BLOCK, BLOCK), (0, 0), (0, 0)))
        tp = tp.reshape(B, nb + 2, BLOCK, N_KV_HEADS, HEAD_DIM)
        return jnp.concatenate([tp[:, :-2], tp[:, 1:-1], tp[:, 2:]], axis=2)

    kw, vw = windows(k), windows(v)
    scores = jnp.einsum('bnqkgd,bnskd->bnkgqs', qb, kw,
                        preferred_element_type=F32) * (HEAD_DIM ** -0.5)
    qi = jnp.arange(BLOCK)[:, None]
    kj = jnp.arange(3 * BLOCK)[None, :]
    dist = jnp.abs(qi - kj + BLOCK)
    key_pos = jnp.arange(nb)[:, None] * BLOCK - BLOCK + jnp.arange(3 * BLOCK)[None, :]
    valid = (dist <= WINDOW)[None] & ((key_pos >= 0) & (key_pos < L))[:, None, :]
    slopes = alibi_slopes().reshape(N_KV_HEADS, GROUP)
    logits = scores - slopes[:, :, None, None] * dist.astype(F32)
    logits = jnp.where(valid[None, :, None, None], logits, -jnp.inf)
    s = sink.astype(F32).reshape(N_KV_HEADS, GROUP)[:, :, None, None]
    m = jnp.maximum(jnp.max(logits, axis=-1, keepdims=True), s)
    p = jnp.exp(logits - m)
    denom = jnp.sum(p, axis=-1, keepdims=True) + jnp.exp(s - m)
    p = (p / denom).astype(v.dtype)
    o = jnp.einsum('bnkgqs,bnskd->bnqkgd', p, vw)
    return o.reshape(B, L, ATTN_WIDTH)


def attention_mixer(x, w_in, sink, w_out):
    B, L, _ = x.shape
    proj = x @ w_in
    q = proj[..., :ATTN_WIDTH].reshape(B, L, N_KV_HEADS, GROUP, HEAD_DIM)
    k = proj[..., ATTN_WIDTH:ATTN_WIDTH + KV_WIDTH].reshape(B, L, N_KV_HEADS, HEAD_DIM)
    v = proj[..., ATTN_WIDTH + KV_WIDTH:ATTN_WIDTH + 2 * KV_WIDTH].reshape(B, L, N_KV_HEADS, HEAD_DIM)
    g = proj[..., ATTN_WIDTH + 2 * KV_WIDTH:]
    o = banded_attention(q, k, v, sink)
    return (o * jax.nn.silu(g)) @ w_out


def trunk(x, ln_g, ln_b, hy_w_in, hy_b_in, hy_w_sc, hy_b_sc, hy_w_f1, hy_b_f1, hy_fr1,
          hy_w_f2, hy_b_f2, hy_fr2, hy_w_f3, hy_b_f3, hy_fr3, hy_w_f4, hy_h_bias,
          hy_w_out, hy_b_out, at_w_in, at_sink, at_w_out):
    for i in range(DEPTH):
        j = i // N_MIXERS
        if i % N_MIXERS == 0:
            h = hyena_mixer(x, hy_w_in[j], hy_b_in[j], hy_w_sc[j], hy_b_sc[j],
                            hy_w_f1[j], hy_b_f1[j], hy_fr1[j], hy_w_f2[j], hy_b_f2[j], hy_fr2[j],
                            hy_w_f3[j], hy_b_f3[j], hy_fr3[j], hy_w_f4[j], hy_h_bias[j],
                            hy_w_out[j], hy_b_out[j])
        else:
            h = attention_mixer(x, at_w_in[j], at_sink[j], at_w_out[j])
        x = layer_norm(DEEPNORM_ALPHA * x + h.astype(x.dtype), ln_g[i], ln_b[i])
    return x


def setup_inputs(seed: int = 0) -> dict:
    key = jax.random.key(seed)
    ks = jax.random.split(key, 24)
    E = HY_WIDTH
    NA, NB = N_HYENA_LAYERS, N_ATTN_LAYERS

    def nrm(k, shape, scale):
        return scale * jax.random.normal(k, shape, F32)

    return {
        "x_prompt": nrm(ks[0], (BATCH, SEQ, D_MODEL), 1.0),
        "x_sample": nrm(ks[1], (DEC_BATCH, DEC_SEQ, D_MODEL), 1.0),
        "ln_g": 1.0 + nrm(ks[2], (DEPTH, D_MODEL), 0.02),
        "ln_b": nrm(ks[3], (DEPTH, D_MODEL), 0.02),
        "hy_w_in": nrm(ks[4], (NA, D_MODEL, 4 * E), D_MODEL ** -0.5),
        "hy_b_in": nrm(ks[5], (NA, 4 * E), 0.02),
        "hy_w_sc": nrm(ks[6], (NA, SHORT_CONV, 3 * E), SHORT_CONV ** -0.5),
        "hy_b_sc": nrm(ks[7], (NA, 3 * E), 0.02),
        "hy_w_f1": nrm(ks[8], (NA, POS_EMB_DIM, FILTER_HIDDEN), POS_EMB_DIM ** -0.5),
        "hy_b_f1": nrm(ks[9], (NA, FILTER_HIDDEN), 0.1),
        "hy_fr1": 1.0 + nrm(ks[10], (NA, FILTER_HIDDEN), 0.1),
        "hy_w_f2": nrm(ks[11], (NA, FILTER_HIDDEN, FILTER_HIDDEN), FILTER_HIDDEN ** -0.5),
        "hy_b_f2": nrm(ks[12], (NA, FILTER_HIDDEN), 0.1),
        "hy_fr2": 1.0 + nrm(ks[13], (NA, FILTER_HIDDEN), 0.1),
        "hy_w_f3": nrm(ks[14], (NA, FILTER_HIDDEN, FILTER_HIDDEN), FILTER_HIDDEN ** -0.5),
        "hy_b_f3": nrm(ks[15], (NA, FILTER_HIDDEN), 0.1),
        "hy_fr3": 1.0 + nrm(ks[16], (NA, FILTER_HIDDEN), 0.1),
        "hy_w_f4": nrm(ks[17], (NA, FILTER_HIDDEN, HY_ORDER * HY_DIRS * E), FILTER_HIDDEN ** -0.5),
        "hy_h_bias": nrm(ks[18], (NA, HY_ORDER, E), 0.5),
        "hy_w_out": nrm(ks[19], (NA, E, D_MODEL), DEEPNORM_BETA * E ** -0.5),
        "hy_b_out": nrm(ks[20], (NA, D_MODEL), 0.02),
        "at_w_in": nrm(ks[21], (NB, D_MODEL, 2 * ATTN_WIDTH + 2 * KV_WIDTH), D_MODEL ** -0.5),
        "at_sink": nrm(ks[22], (NB, N_HEADS), 0.5),
        "at_w_out": nrm(ks[23], (NB, ATTN_WIDTH, D_MODEL), DEEPNORM_BETA * ATTN_WIDTH ** -0.5),
    }


def reference(x_prompt, x_sample, ln_g, ln_b, hy_w_in, hy_b_in, hy_w_sc, hy_b_sc,
              hy_w_f1, hy_b_f1, hy_fr1, hy_w_f2, hy_b_f2, hy_fr2, hy_w_f3, hy_b_f3, hy_fr3,
              hy_w_f4, hy_h_bias, hy_w_out, hy_b_out, at_w_in, at_sink, at_w_out):
    y_prompt = trunk(x_prompt, ln_g, ln_b, hy_w_in, hy_b_in, hy_w_sc, hy_b_sc,
                     hy_w_f1, hy_b_f1, hy_fr1, hy_w_f2, hy_b_f2, hy_fr2, hy_w_f3, hy_b_f3, hy_fr3,
                     hy_w_f4, hy_h_bias, hy_w_out, hy_b_out, at_w_in, at_sink, at_w_out)
    y_sample = trunk(x_sample, ln_g, ln_b, hy_w_in, hy_b_in, hy_w_sc, hy_b_sc,
                     hy_w_f1, hy_b_f1, hy_fr1, hy_w_f2, hy_b_f2, hy_fr2, hy_w_f3, hy_b_f3, hy_fr3,
                     hy_w_f4, hy_h_bias, hy_w_out, hy_b_out, at_w_in, at_sink, at_w_out)
    return (y_prompt, y_sample)
```

```python
import functools
import math

import numpy as np
import jax
import jax.numpy as jnp
from jax import lax
from jax.experimental import pallas as pl
from jax.experimental.pallas import tpu as pltpu

F32 = jnp.float32
BF16 = jnp.bfloat16

N_MIXERS = 2
HY_ORDER = 2
HY_DIRS = 2
SHORT_CONV = 3
POS_EMB_DIM = 33
POS_BANDS = (POS_EMB_DIM - 1) // 2
DECAY_TARGET = 1e-2
FAST_DECAY_PCT = 0.3
SLOW_DECAY_PCT = 1.5
MIN_DECAY = math.log(DECAY_TARGET) / SLOW_DECAY_PCT
MAX_DECAY = math.log(DECAY_TARGET) / FAST_DECAY_PCT
N_KV_HEADS = 4
HEAD_DIM = 128
WINDOW = 128
LN_EPS = 1e-5

FFT_N1 = 128
LANE = 128
VMEM_LIMIT = 52 * 1024 * 1024


def _cparams(*sem):
    return pltpu.CompilerParams(dimension_semantics=sem, vmem_limit_bytes=VMEM_LIMIT)


def _block(m):
    return np.block([[m.real, -m.imag], [m.imag, m.real]])


@functools.lru_cache(maxsize=None)
def _dft_tables(n):
    n1, n2 = FFT_N1, n // FFT_N1
    h1 = n1 // 2
    k1 = np.arange(n1)[:, None]
    ta_data, ta_filt = [], []
    for j in range(n2):
        m = np.exp(-2j * np.pi * (j * k1 / n + np.arange(n1)[None, :] * k1 / n1))
        ta_data.append(_block(m[:, :h1]))
        ta_filt.append(np.concatenate([m.real, m.imag], 0))
    f2 = np.exp(-2j * np.pi * np.arange(n2)[:, None] * np.arange(n2)[None, :] / n2)
    tb_fwd = _block(f2)
    t1 = np.arange(n2)[:, None]
    tb_inv = [_block(np.exp(2j * np.pi * (kk * t1 / n + np.arange(n2)[None, :] * t1 / n2)))
              for kk in range(n1)]
    hc = np.exp(2j * np.pi * np.arange(h1)[:, None] * np.arange(n1)[None, :] / n1) / n
    tc = _block(hc)
    f = lambda a: np.asarray(a, np.float32)
    return f(np.stack(ta_data)), f(np.stack(ta_filt)), f(tb_fwd), f(np.stack(tb_inv)), f(tc)


def _inproj_kernel(x_ref, w_ref, b_ref, o_ref, xb_ref):
    @pl.when(pl.program_id(1) == 0)
    def _():
        xb_ref[...] = x_ref[...].astype(BF16)

    acc = jnp.dot(xb_ref[...], w_ref[...], preferred_element_type=F32) + b_ref[...]
    o_ref[...] = acc.astype(o_ref.dtype)


def _inproj(x, w, b, n_comp, tm=1024, tn=1024):
    m, k = x.shape
    n = w.shape[1]
    wc = n // n_comp
    ncb = wc // tn
    return pl.pallas_call(
        _inproj_kernel,
        grid=(m // tm, n // tn),
        in_specs=[
            pl.BlockSpec((tm, k), lambda i, j: (i, 0)),
            pl.BlockSpec((k, tn), lambda i, j: (0, j)),
            pl.BlockSpec((1, tn), lambda i, j: (0, j)),
        ],
        out_specs=pl.BlockSpec((None, tm, tn), lambda i, j: (j // ncb, i, j % ncb)),
        out_shape=jax.ShapeDtypeStruct((n_comp, m, wc), BF16),
        scratch_shapes=[pltpu.VMEM((tm, k), BF16)],
        compiler_params=_cparams("parallel", "arbitrary"),
        name="inproj",
    )(x, w, b)


def _outproj_ln_kernel(y_ref, w_ref, b_ref, x_ref, g_ref, beta_ref, o_ref, *, alpha):
    h = jnp.dot(y_ref[...], w_ref[...], preferred_element_type=F32) + b_ref[...]
    r = alpha * x_ref[...] + h
    mu = jnp.mean(r, axis=-1, keepdims=True)
    d = r - mu
    var = jnp.mean(d * d, axis=-1, keepdims=True)
    o_ref[...] = d * lax.rsqrt(var + LN_EPS) * g_ref[...] + beta_ref[...]


def _outproj_ln(y, w, b, x, g, beta, alpha, tm=512):
    m, k = y.shape
    d = w.shape[1]
    row = lambda i: (i, 0)
    fixed = lambda i: (0, 0)
    return pl.pallas_call(
        functools.partial(_outproj_ln_kernel, alpha=alpha),
        grid=(m // tm,),
        in_specs=[
            pl.BlockSpec((tm, k), row),
            pl.BlockSpec((k, d), fixed),
            pl.BlockSpec((1, d), fixed),
            pl.BlockSpec((tm, d), row),
            pl.BlockSpec((1, d), fixed),
            pl.BlockSpec((1, d), fixed),
        ],
        out_specs=pl.BlockSpec((tm, d), row),
        out_shape=jax.ShapeDtypeStruct((m, d), F32),
        compiler_params=_cparams("parallel"),
        name="outproj_ln",
    )(y, w, b, x, g, beta)


SC_HALO = 16


def _shortconv_kernel(cur_ref, prev_ref, next_ref, w_ref, b_ref, o_ref):
    i = pl.program_id(2)
    last = pl.num_programs(2) - 1
    cur = cur_ref[...].astype(F32)
    tl = cur.shape[0]
    prev_row = jnp.where(i > 0, prev_ref[SC_HALO - 1:SC_HALO, :].astype(F32), 0.0)
    next_row = jnp.where(i < last, next_ref[0:1, :].astype(F32), 0.0)
    row = lax.broadcasted_iota(jnp.int32, cur.shape, 0)
    down = jnp.where(row == 0, prev_row, pltpu.roll(cur, 1, axis=0))
    up = jnp.where(row == tl - 1, next_row, pltpu.roll(cur, tl - 1, axis=0))
    out = b_ref[...] + down * w_ref[0:1, :] + cur * w_ref[1:2, :] + up * w_ref[2:3, :]
    o_ref[...] = out.astype(o_ref.dtype)


def _shortconv(proj, w_sc, b_sc, tl=1024, tc=1024):
    _, bsz, l, e = proj.shape
    ncb = e // tc
    nhb = l // SC_HALO
    step = tl // SC_HALO
    return pl.pallas_call(
        _shortconv_kernel,
        grid=(3, bsz, l // tl, ncb),
        in_specs=[
            pl.BlockSpec((None, None, tl, tc), lambda c, b, i, j: (c, b, i, j)),
            pl.BlockSpec((None, None, SC_HALO, tc),
                         lambda c, b, i, j: (c, b, jnp.maximum(i * step - 1, 0), j)),
            pl.BlockSpec((None, None, SC_HALO, tc),
                         lambda c, b, i, j: (c, b, jnp.minimum((i + 1) * step, nhb - 1), j)),
            pl.BlockSpec((SHORT_CONV, tc), lambda c, b, i, j: (0, c * ncb + j)),
            pl.BlockSpec((1, tc), lambda c, b, i, j: (0, c * ncb + j)),
        ],
        out_specs=pl.BlockSpec((None, None, tl, tc), lambda c, b, i, j: (c, b, i, j)),
        out_shape=jax.ShapeDtypeStruct((3, bsz, l, e), BF16),
        compiler_params=_cparams("parallel", "parallel", "arbitrary", "arbitrary"),
        name="shortconv",
    )(proj, proj, proj, w_sc, b_sc)


def _filter_kernel(feat_ref, w1_ref, b1_ref, fr1_ref, w2_ref, b2_ref, fr2_ref, w3_ref, b3_ref,
                   fr3_ref, w4_ref, delta_ref, k_ref, ss_ref, h_ref):
    half = pl.program_id(0)
    hp = lax.Precision.HIGHEST

    @pl.when((pl.program_id(1) == 0) & (pl.program_id(2) == 0))
    def _():
        h = jnp.sin(fr1_ref[...] * (jnp.dot(feat_ref[...], w1_ref[...], precision=hp,
                                             preferred_element_type=F32) + b1_ref[...]))
        h = jnp.sin(fr2_ref[...] * (jnp.dot(h, w2_ref[...], precision=hp,
                                             preferred_element_type=F32) + b2_ref[...]))
        h = jnp.sin(fr3_ref[...] * (jnp.dot(h, w3_ref[...], precision=hp,
                                             preferred_element_type=F32) + b3_ref[...]))
        h_ref[...] = h

    k = jnp.dot(h_ref[...], w4_ref[...], precision=hp, preferred_element_type=F32)
    t_norm = feat_ref[:, 0:1]
    k = k * jnp.exp(-t_norm * delta_ref[...])
    row = lax.broadcasted_iota(jnp.int32, k.shape, 0)
    k = jnp.where((row == 0) & (half == 1), 0.0, k)
    k_ref[...] = k.astype(k_ref.dtype)
    ss_ref[...] = jnp.sum(k * k, axis=0, keepdims=True)


def _filters(l, e, w_f1, b_f1, fr1, w_f2, b_f2, fr2, w_f3, b_f3, fr3, w_f4, ec=512):
    fh = w_f1.shape[1]
    t_norm = jnp.linspace(0.0, 1.0, l, dtype=F32)
    w = 2.0 * math.pi * jnp.arange(l, dtype=F32) / l
    f = jnp.linspace(1e-4, POS_BANDS - 1, POS_BANDS, dtype=F32)
    ang = w[:, None] * f[None, :]
    feat = jnp.concatenate([t_norm[:, None], jnp.cos(ang), -jnp.sin(ang)], axis=-1)
    feat_rev = jnp.concatenate([feat[:1], feat[:0:-1]], axis=0)
    feat2 = jnp.pad(jnp.concatenate([feat, feat_rev], axis=0), ((0, 0), (0, LANE - POS_EMB_DIM)))
    w1p = jnp.pad(w_f1, ((0, LANE - POS_EMB_DIM), (0, 0)))
    deltas = jnp.abs(jnp.linspace(MIN_DECAY, MAX_DECAY, e, dtype=F32))[None, :]
    nec = e // ec
    vec = lambda a: a.reshape(1, -1)
    fixed = lambda h, o, j: (0, 0)
    return pl.pallas_call(
        _filter_kernel,
        grid=(HY_DIRS, HY_ORDER, nec),
        in_specs=[
            pl.BlockSpec((l, LANE), lambda h, o, j: (h, 0)),
            pl.BlockSpec((LANE, fh), fixed), pl.BlockSpec((1, fh), fixed), pl.BlockSpec((1, fh), fixed),
            pl.BlockSpec((fh, fh), fixed), pl.BlockSpec((1, fh), fixed), pl.BlockSpec((1, fh), fixed),
            pl.BlockSpec((fh, fh), fixed), pl.BlockSpec((1, fh), fixed), pl.BlockSpec((1, fh), fixed),
            pl.BlockSpec((fh, ec), lambda h, o, j: (0, (o * HY_DIRS + h) * nec + j)),
            pl.BlockSpec((1, ec), lambda h, o, j: (0, j)),
        ],
        out_specs=[
            pl.BlockSpec((None, l, ec), lambda h, o, j: (o, h, j)),
            pl.BlockSpec((None, None, 1, ec), lambda h, o, j: (o, h, 0, j)),
        ],
        out_shape=[
            jax.ShapeDtypeStruct((HY_ORDER, 2 * l, e), BF16),
            jax.ShapeDtypeStruct((HY_ORDER, HY_DIRS, 1, e), F32),
        ],
        scratch_shapes=[pltpu.VMEM((l, fh), F32)],
        compiler_params=_cparams("arbitrary", "arbitrary", "arbitrary"),
        name="hyena_filter",
    )(feat2, w1p, vec(b_f1), vec(fr1), w_f2, vec(b_f2), vec(fr2), w_f3, vec(b_f3), vec(fr3),
      w_f4, deltas)


def _fft_a_kernel(z_ref, ta_ref, are_ref, aim_ref, *, g, e):
    n1 = are_ref.shape[1]
    for j in range(g):
        z = z_ref[:, :, j * e:(j + 1) * e]
        z = z.reshape(z.shape[0] * z.shape[1], e)
        r = jnp.dot(ta_ref[j], z, preferred_element_type=F32)
        are_ref[j] = r[:n1].astype(BF16)
        aim_ref[j] = r[n1:].astype(BF16)


def _fft_a(z5, comp, ta, g=2):
    _, p, two, h1, ne = z5.shape
    n2, m2, kk = ta.shape
    n1 = m2 // 2
    e = ne // n2
    out = jax.ShapeDtypeStruct((p, n2, n1, e), BF16)
    ospec = pl.BlockSpec((None, g, n1, e), lambda pi, j: (pi, j, 0, 0))
    return pl.pallas_call(
        functools.partial(_fft_a_kernel, g=g, e=e),
        grid=(p, n2 // g),
        in_specs=[
            pl.BlockSpec((None, None, two, h1, g * e), lambda pi, j: (comp, pi, 0, 0, j)),
            pl.BlockSpec((g, m2, kk), lambda pi, j: (j, 0, 0)),
        ],
        out_specs=[ospec, ospec],
        out_shape=[out, out],
        compiler_params=_cparams("parallel", "arbitrary"),
        name="fft_a",
    )(z5, ta)


def _fft_bf_kernel(are_ref, aim_ref, tf_ref, ss_ref, kre_ref, kim_ref, *, g, e):
    n2 = are_ref.shape[0]
    rs = lax.rsqrt(ss_ref[0] + ss_ref[1] + 1e-12)
    for j in range(g):
        sl = slice(j * e, (j + 1) * e)
        a = jnp.concatenate([are_ref[:, sl], aim_ref[:, sl]], axis=0)
        x = jnp.dot(tf_ref[...], a, preferred_element_type=F32) * rs
        kre_ref[j] = x[:n2].astype(kre_ref.dtype)
        kim_ref[j] = x[n2:].astype(kim_ref.dtype)


def _fft_b_filter(are, aim, tf, ss, g=2):
    p, n2, n1, e = are.shape
    a2 = are.reshape(p, n2, n1 * e)
    b2 = aim.reshape(p, n2, n1 * e)
    ispec = pl.BlockSpec((None, n2, g * e), lambda j, pi: (pi, 0, j))
    ospec = pl.BlockSpec((None, g, n2, e), lambda j, pi: (pi, j, 0, 0))
    out = jax.ShapeDtypeStruct((p, n1, n2, e), BF16)
    return pl.pallas_call(
        functools.partial(_fft_bf_kernel, g=g, e=e),
        grid=(n1 // g, p),
        in_specs=[ispec, ispec,
                  pl.BlockSpec((2 * n2, 2 * n2), lambda j, pi: (0, 0)),
                  pl.BlockSpec((None, HY_DIRS, 1, e), lambda j, pi: (pi, 0, 0, 0))],
        out_specs=[ospec, ospec],
        out_shape=[out, out],
        compiler_params=_cparams("parallel", "arbitrary"),
        name="fft_b_filter",
    )(a2, b2, tf, ss)


def _fft_b_kernel(are_ref, aim_ref, kre_ref, kim_ref, tf_ref, tg_ref, cre_ref, cim_ref, *, g, e):
    n2 = are_ref.shape[0]
    for j in range(g):
        sl = slice(j * e, (j + 1) * e)
        a = jnp.concatenate([are_ref[:, sl], aim_ref[:, sl]], axis=0)
        x = jnp.dot(tf_ref[...], a, preferred_element_type=F32)
        xr, xi = x[:n2], x[n2:]
        kr = kre_ref[j].astype(F32)
        ki = kim_ref[j].astype(F32)
        y = jnp.concatenate([xr * kr - xi * ki, xr * ki + xi * kr], axis=0).astype(BF16)
        c = jnp.dot(tg_ref[j], y, preferred_element_type=F32)
        cre_ref[:, sl] = c[:n2].astype(BF16)
        cim_ref[:, sl] = c[n2:].astype(BF16)


def _fft_b(are, aim, kre, kim, order, tf, tg, g=2):
    p, n2, n1, e = are.shape
    a2 = are.reshape(p, n2, n1 * e)
    b2 = aim.reshape(p, n2, n1 * e)
    dspec = pl.BlockSpec((None, n2, g * e), lambda j, pi: (pi, 0, j))
    kspec = pl.BlockSpec((None, g, n2, e), lambda j, pi: (order, j, 0, 0))
    out = jax.ShapeDtypeStruct((p, n2, n1 * e), BF16)
    cre, cim = pl.pallas_call(
        functools.partial(_fft_b_kernel, g=g, e=e),
        grid=(n1 // g, p),
        in_specs=[dspec, dspec, kspec, kspec,
                  pl.BlockSpec((2 * n2, 2 * n2), lambda j, pi: (0, 0)),
                  pl.BlockSpec((g, 2 * n2, 2 * n2), lambda j, pi: (j, 0, 0))],
        out_specs=[dspec, dspec],
        out_shape=[out, out],
        compiler_params=_cparams("parallel", "arbitrary"),
        name="fft_b",
    )(a2, b2, kre, kim, tf, tg)
    return cre.reshape(p, n2, n1, e), cim.reshape(p, n2, n1, e)


def _fft_c_conv(cre_ref, cim_ref, tc_ref, j):
    c = jnp.concatenate([cre_ref[j], cim_ref[j]], axis=0)
    return jnp.dot(tc_ref[...], c, preferred_element_type=F32)


def _fft_c1_kernel(cre_ref, cim_ref, tc_ref, z_ref, gate_ref, hb_ref, ta_ref,
                   z1_ref, are_ref, aim_ref, *, g, e):
    n1 = are_ref.shape[1]
    two, h1 = z_ref.shape[0], z_ref.shape[1]
    for j in range(g):
        sl = slice(j * e, (j + 1) * e)
        y = _fft_c_conv(cre_ref, cim_ref, tc_ref, j)
        z0 = z_ref[:, :, sl].reshape(two * h1, e).astype(F32)
        gate = gate_ref[:, :, sl].reshape(two * h1, e).astype(F32)
        z1 = (gate * (y + hb_ref[...] * z0)).astype(BF16)
        z1_ref[:, :, sl] = z1.reshape(two, h1, e)
        r = jnp.dot(ta_ref[j], z1, preferred_element_type=F32)
        are_ref[j] = r[:n1].astype(BF16)
        aim_ref[j] = r[n1:].astype(BF16)


def _fft_c2_kernel(cre_ref, cim_ref, tc_ref, z_ref, gate_ref, hb_ref, sg_ref, y_ref, *, g, e):
    two, h1 = z_ref.shape[0], z_ref.shape[1]
    for j in range(g):
        sl = slice(j * e, (j + 1) * e)
        y = _fft_c_conv(cre_ref, cim_ref, tc_ref, j)
        z1 = z_ref[:, :, sl].reshape(two * h1, e).astype(F32)
        gate = gate_ref[:, :, sl].reshape(two * h1, e).astype(F32)
        z2 = gate * (y + hb_ref[...] * z1)
        sg = sg_ref[:, :, sl].reshape(two * h1, e).astype(F32)
        out = z2 * (sg / (1.0 + jnp.exp(-sg)))
        y_ref[:, :, sl] = out.astype(BF16).reshape(two, h1, e)


def _fft_c1(cre, cim, tc, u5, z_comp, gate_comp, hb, ta, g=2):
    p, n2, n1, e = cre.shape
    _, _, two, h1, ne = u5.shape
    cspec = pl.BlockSpec((None, g, n1, e), lambda pi, j: (pi, j, 0, 0))
    zspec = lambda comp: pl.BlockSpec((None, None, two, h1, g * e),
                                      lambda pi, j: (comp, pi, 0, 0, j))
    aout = jax.ShapeDtypeStruct((p, n2, n1, e), BF16)
    return pl.pallas_call(
        functools.partial(_fft_c1_kernel, g=g, e=e),
        grid=(p, n2 // g),
        in_specs=[cspec, cspec,
                  pl.BlockSpec(tc.shape, lambda pi, j: (0, 0)),
                  zspec(z_comp), zspec(gate_comp),
                  pl.BlockSpec((1, e), lambda pi, j: (0, 0)),
                  pl.BlockSpec((g,) + ta.shape[1:], lambda pi, j: (j, 0, 0))],
        out_specs=[pl.BlockSpec((None, two, h1, g * e), lambda pi, j: (pi, 0, 0, j)), cspec, cspec],
        out_shape=[jax.ShapeDtypeStruct((p, two, h1, ne), BF16), aout, aout],
        compiler_params=_cparams("parallel", "arbitrary"),
        name="fft_c1",
    )(cre, cim, tc, u5, u5, hb, ta)


def _fft_c2(cre, cim, tc, z1, u5, gate_comp, hb, p5, sg_comp, g=2):
    p, n2, n1, e = cre.shape
    _, _, two, h1, ne = u5.shape
    cspec = pl.BlockSpec((None, g, n1, e), lambda pi, j: (pi, j, 0, 0))
    zspec = lambda comp: pl.BlockSpec((None, None, two, h1, g * e),
                                      lambda pi, j: (comp, pi, 0, 0, j))
    z1spec = pl.BlockSpec((None, two, h1, g * e), lambda pi, j: (pi, 0, 0, j))
    return pl.pallas_call(
        functools.partial(_fft_c2_kernel, g=g, e=e),
        grid=(p, n2 // g),
        in_specs=[cspec, cspec,
                  pl.BlockSpec(tc.shape, lambda pi, j: (0, 0)),
                  z1spec, zspec(gate_comp),
                  pl.BlockSpec((1, e), lambda pi, j: (0, 0)),
                  zspec(sg_comp)],
        out_specs=z1spec,
        out_shape=jax.ShapeDtypeStruct((p, two, h1, ne), BF16),
        compiler_params=_cparams("parallel", "arbitrary"),
        name="fft_c2",
    )(cre, cim, tc, z1, u5, hb, p5)


def _attn_kernel(slope_ref, sink_ref, q_ref, kp_ref, kc_ref, kn_ref, vp_ref, vc_ref, vn_ref,
                 g_ref, o_ref, *, group, scale):
    n = pl.program_id(1)
    nb = pl.num_programs(1)
    kvh = pl.program_id(2)
    blk, hd = kc_ref.shape
    k = jnp.concatenate([kp_ref[...], kc_ref[...], kn_ref[...]], axis=0)
    v = jnp.concatenate([vp_ref[...], vc_ref[...], vn_ref[...]], axis=0)
    qi = lax.broadcasted_iota(jnp.int32, (blk, 3 * blk), 0)
    kj = lax.broadcasted_iota(jnp.int32, (blk, 3 * blk), 1)
    dist = jnp.abs(qi - kj + blk)
    valid = (dist <= WINDOW) & ((kj >= blk) | (n > 0)) & ((kj < 2 * blk) | (n < nb - 1))
    distf = dist.astype(F32)
    for gi in range(group):
        h = kvh * group + gi
        sl = slice(gi * hd, (gi + 1) * hd)
        s = lax.dot_general(q_ref[:, sl], k, (((1,), (1,)), ((), ())),
                            preferred_element_type=F32) * scale
        logits = jnp.where(valid, s - slope_ref[h] * distf, -jnp.inf)
        sk = sink_ref[h]
        m = jnp.maximum(jnp.max(logits, axis=-1, keepdims=True), sk)
        p = jnp.exp(logits - m)
        denom = jnp.sum(p, axis=-1, keepdims=True) + jnp.exp(sk - m)
        pn = (p * (1.0 / denom)).astype(BF16)
        o = jnp.dot(pn, v, preferred_element_type=F32)
        gg = g_ref[:, sl].astype(F32)
        o_ref[:, sl] = (o * (gg / (1.0 + jnp.exp(-gg)))).astype(o_ref.dtype)


def _attention(proj, slopes, sink, n_heads):
    bsz, l, _ = proj.shape
    hd, blk = HEAD_DIM, WINDOW
    group = n_heads // N_KV_HEADS
    aw = n_heads * hd
    qw = group * hd
    nb = l // blk
    k0 = aw // hd
    v0 = k0 + N_KV_HEADS
    g0 = (aw + 2 * N_KV_HEADS * hd) // qw
    smem = pl.BlockSpec(memory_space=pltpu.SMEM)
    prev = lambda n: jnp.maximum(n - 1, 0)
    nxt = lambda n: jnp.minimum(n + 1, nb - 1)
    kv = lambda c0, f: pl.BlockSpec((None, blk, hd), lambda b, n, h: (b, f(n), c0 + h))
    same = lambda n: n
    return pl.pallas_call(
        functools.partial(_attn_kernel, group=group, scale=hd ** -0.5),
        grid=(bsz, nb, N_KV_HEADS),
        in_specs=[smem, smem,
                  pl.BlockSpec((None, blk, qw), lambda b, n, h: (b, n, h)),
                  kv(k0, prev), kv(k0, same), kv(k0, nxt),
                  kv(v0, prev), kv(v0, same), kv(v0, nxt),
                  pl.BlockSpec((None, blk, qw), lambda b, n, h: (b, n, g0 + h))],
        out_specs=pl.BlockSpec((None, blk, qw), lambda b, n, h: (b, n, h)),
        out_shape=jax.ShapeDtypeStruct((bsz, l, aw), BF16),
        compiler_params=_cparams("parallel", "arbitrary", "arbitrary"),
        name="swa_attention",
    )(slopes, sink, proj, proj, proj, proj, proj, proj, proj, proj)


def _hyena_filter_spectrum(l, e, hy, tabs):
    _, ta_filt, tb_fwd, _, _ = tabs
    k_un, ss = _filters(l, e, hy["w_f1"], hy["b_f1"], hy["fr1"], hy["w_f2"], hy["b_f2"], hy["fr2"],
                        hy["w_f3"], hy["b_f3"], hy["fr3"], hy["w_f4"])
    n1 = FFT_N1
    n2 = 2 * l // n1
    k5 = k_un.reshape(1, HY_ORDER, 2, n1 // 2, n2 * e)
    are, aim = _fft_a(k5, 0, ta_filt)
    return _fft_b_filter(are, aim, tb_fwd, ss)


def _hyena_layer(x, hy, kf, tabs, ln_g, ln_b, alpha):
    bsz, l, d = x.shape
    e = hy["w_out"].shape[0]
    ta_data, _, tb_fwd, tb_inv, tc = tabs
    kre, kim = kf
    n1 = FFT_N1
    n2 = 2 * l // n1
    x2 = x.reshape(bsz * l, d)
    proj = _inproj(x2, hy["w_in"], hy["b_in"], 4)
    proj = proj.reshape(4, bsz, l, e)
    u = _shortconv(proj, hy["w_sc"], hy["b_sc"])
    shape5 = (bsz // 2, 2, n1 // 2, n2 * e)
    u5 = u.reshape((3,) + shape5)
    p5 = proj.reshape((4,) + shape5)
    are, aim = _fft_a(u5, 0, ta_data)
    cre, cim = _fft_b(are, aim, kre, kim, 0, tb_fwd, tb_inv)
    z1, are, aim = _fft_c1(cre, cim, tc, u5, 0, 1, hy["h_bias"][0:1], ta_data)
    cre, cim = _fft_b(are, aim, kre, kim, 1, tb_fwd, tb_inv)
    y = _fft_c2(cre, cim, tc, z1, u5, 2, hy["h_bias"][1:2], p5, 3)
    y = y.reshape(bsz * l, e)
    out = _outproj_ln(y, hy["w_out"], hy["b_out"], x2, ln_g, ln_b, alpha)
    return out.reshape(bsz, l, d)


def _attention_layer(x, at, ln_g, ln_b, alpha):
    bsz, l, d = x.shape
    n_heads = at["sink"].shape[0]
    x2 = x.reshape(bsz * l, d)
    proj = _inproj(x2, at["w_in"], at["b_in"], 1)
    proj = proj.reshape(bsz, l, -1)
    slopes = jnp.exp2(-8.0 * jnp.arange(1, n_heads + 1, dtype=F32) / n_heads)
    o = _attention(proj, slopes, at["sink"], n_heads)
    out = _outproj_ln(o.reshape(bsz * l, -1), at["w_out"], at["b_out"], x2, ln_g, ln_b, alpha)
    return out.reshape(bsz, l, d)


def kernel(x_prompt, x_sample, ln_g, ln_b, hy_w_in, hy_b_in, hy_w_sc, hy_b_sc, hy_w_f1, hy_b_f1,
           hy_fr1, hy_w_f2, hy_b_f2, hy_fr2, hy_w_f3, hy_b_f3, hy_fr3, hy_w_f4, hy_h_bias,
           hy_w_out, hy_b_out, at_w_in, at_sink, at_w_out):
    depth = ln_g.shape[0]
    alpha = (2 * depth) ** 0.25
    l = x_prompt.shape[1]
    d = x_prompt.shape[2]
    assert x_sample.shape[1] == l and l % (FFT_N1 // 2) == 0
    assert x_prompt.shape[0] % 2 == 0 and x_sample.shape[0] % 2 == 0
    tabs = tuple(jnp.asarray(t).astype(BF16) for t in _dft_tables(2 * l))
    row = lambda a: a.reshape(1, -1)

    layers = []
    for i in range(depth):
        j = i // N_MIXERS
        if i % N_MIXERS == 0:
            e = hy_w_out.shape[1]
            hy = dict(w_in=hy_w_in[j].astype(BF16), b_in=row(hy_b_in[j]), w_sc=hy_w_sc[j],
                      b_sc=row(hy_b_sc[j]), w_f1=hy_w_f1[j], b_f1=hy_b_f1[j], fr1=hy_fr1[j],
                      w_f2=hy_w_f2[j], b_f2=hy_b_f2[j], fr2=hy_fr2[j], w_f3=hy_w_f3[j],
                      b_f3=hy_b_f3[j], fr3=hy_fr3[j], w_f4=hy_w_f4[j], h_bias=hy_h_bias[j],
                      w_out=hy_w_out[j].astype(BF16), b_out=row(hy_b_out[j]))
            kf = _hyena_filter_spectrum(l, e, hy, tabs)
            layers.append(("hyena", hy, kf))
        else:
            at = dict(w_in=at_w_in[j].astype(BF16),
                      b_in=jnp.zeros((1, at_w_in.shape[2]), F32),
                      sink=at_sink[j], w_out=at_w_out[j].astype(BF16),
                      b_out=jnp.zeros((1, d), F32))
            layers.append(("attn", at, None))

    def trunk(x):
        for i, (kind, prm, kf) in enumerate(layers):
            g, b = row(ln_g[i]), row(ln_b[i])
            if kind == "hyena":
                x = _hyena_layer(x, prm, kf, tabs, g, b, alpha)
            else:
                x = _attention_layer(x, prm, g, b, alpha)
        return x

    return (trunk(x_prompt), trunk(x_sample))
```

```python
import functools
import math

import numpy as np
import jax
import jax.numpy as jnp
from jax import lax
from jax.experimental import pallas as pl
from jax.experimental.pallas import tpu as pltpu

F32 = jnp.float32
BF16 = jnp.bfloat16

N_MIXERS = 2
HY_ORDER = 2
HY_DIRS = 2
SHORT_CONV = 3
POS_EMB_DIM = 33
POS_BANDS = (POS_EMB_DIM - 1) // 2
DECAY_TARGET = 1e-2
FAST_DECAY_PCT = 0.3
SLOW_DECAY_PCT = 1.5
MIN_DECAY = math.log(DECAY_TARGET) / SLOW_DECAY_PCT
MAX_DECAY = math.log(DECAY_TARGET) / FAST_DECAY_PCT
N_KV_HEADS = 4
HEAD_DIM = 128
WINDOW = 128
LN_EPS = 1e-5
LOG2E = 1.4426950408889634
MASK_DIST = 1e30

FFT_N1 = 128
LANE = 128
BF16_ROWS = 16
F32_ROWS = 8
VMEM_LIMIT = 56 * 1024 * 1024


def _cparams(*sem):
    return pltpu.CompilerParams(dimension_semantics=sem, vmem_limit_bytes=VMEM_LIMIT)


def _swap_major_sublane(x):
    return pltpu.einshape("jkc->kjc", x)


def _block(m):
    return np.block([[m.real, -m.imag], [m.imag, m.real]])


@functools.lru_cache(maxsize=None)
def _dft_tables(n):
    n1, n2 = FFT_N1, n // FFT_N1
    h1 = n1 // 2
    k1 = np.arange(n1)[:, None]
    ta_data, ta_filt = [], []
    for j in range(n2):
        m = np.exp(-2j * np.pi * (j * k1 / n + np.arange(n1)[None, :] * k1 / n1))
        ta_data.append(_block(m[:, :h1]))
        ta_filt.append(np.concatenate([m.real, m.imag], 0))
    f2 = np.exp(-2j * np.pi * np.arange(n2)[:, None] * np.arange(n2)[None, :] / n2)
    tb_fwd = _block(f2)
    t1 = np.arange(n2)[:, None]
    tb_inv = [_block(np.exp(2j * np.pi * (kk * t1 / n + np.arange(n2)[None, :] * t1 / n2)))
              for kk in range(n1)]
    hc = np.exp(2j * np.pi * np.arange(h1)[:, None] * np.arange(n1)[None, :] / n1) / n
    tc = _block(hc)
    f = lambda a: np.asarray(a, np.float32)
    return f(np.stack(ta_data)), f(np.stack(ta_filt)), f(tb_fwd), f(np.stack(tb_inv)), f(tc)


def _inproj_kernel(x_ref, w_ref, b_ref, cs_ref, o_ref, xb_ref, *, scaled):
    @pl.when(pl.program_id(1) == 0)
    def _():
        xb_ref[...] = x_ref[...].astype(BF16)

    acc = jnp.dot(xb_ref[...], w_ref[...], preferred_element_type=F32)
    if scaled:
        acc = acc * cs_ref[...]
    acc = acc + b_ref[...]
    if len(o_ref.shape) == 3:
        n2, n1b, tn = o_ref.shape
        acc = _swap_major_sublane(acc.reshape(n1b, n2, tn))
    o_ref[...] = acc.astype(o_ref.dtype)


def _inproj(x, w, b, n_comp, col_scale=None, permute=None, tm=1024, tn=1024):
    m, k = x.shape
    n = w.shape[1]
    wc = n // n_comp
    tn = min(tn, wc)
    ncb = wc // tn
    scaled = col_scale is not None
    if not scaled:
        col_scale = b
    if permute is None:
        out_spec = pl.BlockSpec((None, tm, tn), lambda i, j: (j // ncb, i, j % ncb))
        out_shape = jax.ShapeDtypeStruct((n_comp, m, wc), BF16)
    else:
        l, n2 = permute
        tm = BF16_ROWS * n2
        tpb = l // tm
        out_spec = pl.BlockSpec((None, None, n2, BF16_ROWS, tn),
                                lambda i, j: (j // ncb, i // tpb, 0, i % tpb, j % ncb))
        out_shape = jax.ShapeDtypeStruct((n_comp, m // l, n2, l // n2, wc), BF16)
    vec = pl.BlockSpec((1, tn), lambda i, j: (0, j))
    return pl.pallas_call(
        functools.partial(_inproj_kernel, scaled=scaled),
        grid=(m // tm, n // tn),
        in_specs=[
            pl.BlockSpec((tm, k), lambda i, j: (i, 0)),
            pl.BlockSpec((k, tn), lambda i, j: (0, j)),
            vec, vec,
        ],
        out_specs=out_spec,
        out_shape=out_shape,
        scratch_shapes=[pltpu.VMEM((tm, k), BF16)],
        compiler_params=_cparams("parallel", "arbitrary"),
        name="inproj",
    )(x, w, b, col_scale)


def _outproj_ln_kernel(y_ref, w_ref, b_ref, x_ref, g_ref, beta_ref, o_ref, *, alpha, permuted):
    y = y_ref[...]
    if permuted:
        nb, na, kk = y.shape
        y = y.reshape(nb * na, kk)
    h = jnp.dot(y, w_ref[...], preferred_element_type=F32) + b_ref[...]
    if permuted:
        h = _swap_major_sublane(h.reshape(nb, na, h.shape[-1]))
    r = alpha * x_ref[...] + h
    mu = jnp.mean(r, axis=-1, keepdims=True)
    d = r - mu
    var = jnp.mean(d * d, axis=-1, keepdims=True)
    o_ref[...] = d * lax.rsqrt(var + LN_EPS) * g_ref[...] + beta_ref[...]


def _outproj_ln(y, w, b, x, g, beta, alpha, tm=512):
    m, k = y.shape
    d = w.shape[1]
    row = lambda i: (i, 0)
    fixed = lambda i: (0, 0)
    return pl.pallas_call(
        functools.partial(_outproj_ln_kernel, alpha=alpha, permuted=False),
        grid=(m // tm,),
        in_specs=[
            pl.BlockSpec((tm, k), row),
            pl.BlockSpec((k, d), fixed),
            pl.BlockSpec((1, d), fixed),
            pl.BlockSpec((tm, d), row),
            pl.BlockSpec((1, d), fixed),
            pl.BlockSpec((1, d), fixed),
        ],
        out_specs=pl.BlockSpec((tm, d), row),
        out_shape=jax.ShapeDtypeStruct((m, d), F32),
        compiler_params=_cparams("parallel"),
        name="outproj_ln",
    )(y, w, b, x, g, beta)


def _outproj_ln_permuted(y, w, b, x, g, beta, alpha):
    bsz, n2, h1, k = y.shape
    d = w.shape[1]
    nb = F32_ROWS
    fixed = lambda bi, j: (0, 0)
    xspec = pl.BlockSpec((None, h1, nb, d), lambda bi, j: (bi, 0, j, 0))
    return pl.pallas_call(
        functools.partial(_outproj_ln_kernel, alpha=alpha, permuted=True),
        grid=(bsz, n2 // nb),
        in_specs=[
            pl.BlockSpec((None, nb, h1, k), lambda bi, j: (bi, j, 0, 0)),
            pl.BlockSpec((k, d), fixed),
            pl.BlockSpec((1, d), fixed),
            xspec,
            pl.BlockSpec((1, d), fixed),
            pl.BlockSpec((1, d), fixed),
        ],
        out_specs=xspec,
        out_shape=jax.ShapeDtypeStruct((bsz, h1, n2, d), F32),
        compiler_params=_cparams("parallel", "parallel"),
        name="outproj_ln_perm",
    )(y, w, b, x, g, beta)


def _shortconv_kernel(cur_ref, prev_ref, next_ref, w_ref, b_ref, o_ref):
    i = pl.program_id(2)
    last = pl.num_programs(2) - 1
    n2, n1b, tc = cur_ref.shape
    cur = cur_ref[...].astype(F32)
    row = lax.broadcasted_iota(jnp.int32, (n1b, tc), 0)
    prev_row = jnp.where(i > 0, prev_ref[0, n1b - 1:n1b, :].astype(F32), 0.0)
    lo = jnp.where(row == 0, prev_row, pltpu.roll(cur[n2 - 1], 1, axis=0))
    next_row = jnp.where(i < last, next_ref[0, 0:1, :].astype(F32), 0.0)
    hi = jnp.where(row == n1b - 1, next_row, pltpu.roll(cur[0], n1b - 1, axis=0))
    down = jnp.concatenate([lo[None], cur[:n2 - 1]], axis=0)
    up = jnp.concatenate([cur[1:], hi[None]], axis=0)
    out = b_ref[...] + down * w_ref[0:1, :] + cur * w_ref[1:2, :] + up * w_ref[2:3, :]
    o_ref[...] = out.astype(o_ref.dtype)


def _shortconv(proj, w_sc, b_sc, tc=1024):
    _, bsz, n2, h1, e = proj.shape
    ncomp = w_sc.shape[1] // e
    n1b = BF16_ROWS
    tc = min(tc, e)
    ncb = e // tc
    nblk = h1 // n1b
    wcol = lambda c, b, i, j: (0, c * ncb + j)
    blk = lambda c, b, i, j: (c, b, 0, i, j)
    return pl.pallas_call(
        _shortconv_kernel,
        grid=(ncomp, bsz, nblk, ncb),
        in_specs=[
            pl.BlockSpec((None, None, n2, n1b, tc), blk),
            pl.BlockSpec((None, None, 1, n1b, tc),
                         lambda c, b, i, j: (c, b, n2 - 1, jnp.maximum(i - 1, 0), j)),
            pl.BlockSpec((None, None, 1, n1b, tc),
                         lambda c, b, i, j: (c, b, 0, jnp.minimum(i + 1, nblk - 1), j)),
            pl.BlockSpec((SHORT_CONV, tc), wcol),
            pl.BlockSpec((1, tc), wcol),
        ],
        out_specs=pl.BlockSpec((None, None, n2, n1b, tc), blk),
        out_shape=jax.ShapeDtypeStruct((ncomp, bsz, n2, h1, e), BF16),
        compiler_params=_cparams("parallel", "parallel", "arbitrary", "arbitrary"),
        name="shortconv",
    )(proj, proj, proj, w_sc, b_sc)


def _filter_kernel(feat_ref, w1_ref, b1_ref, fr1_ref, w2_ref, b2_ref, fr2_ref, w3_ref, b3_ref,
                   fr3_ref, w4_ref, delta_ref, k_ref, ss_ref, h_ref):
    half = pl.program_id(0)
    hp = lax.Precision.HIGHEST

    @pl.when((pl.program_id(1) == 0) & (pl.program_id(2) == 0))
    def _():
        h = jnp.sin(fr1_ref[...] * (jnp.dot(feat_ref[...], w1_ref[...], precision=hp,
                                             preferred_element_type=F32) + b1_ref[...]))
        h = jnp.sin(fr2_ref[...] * (jnp.dot(h, w2_ref[...], precision=hp,
                                             preferred_element_type=F32) + b2_ref[...]))
        h = jnp.sin(fr3_ref[...] * (jnp.dot(h, w3_ref[...], precision=hp,
                                             preferred_element_type=F32) + b3_ref[...]))
        h_ref[...] = h

    k = jnp.dot(h_ref[...], w4_ref[...], precision=hp, preferred_element_type=F32)
    t_norm = feat_ref[:, 0:1]
    k = k * jnp.exp(-t_norm * delta_ref[...])
    row = lax.broadcasted_iota(jnp.int32, k.shape, 0)
    k = jnp.where((row == 0) & (half == 1), 0.0, k)
    ss_ref[...] = jnp.sum(k * k, axis=0, keepdims=True)
    n2, rows, ec = k_ref.shape
    k_ref[...] = _swap_major_sublane(k.reshape(rows, n2, ec)).astype(k_ref.dtype)


def _filters(l, e, n2, w_f1, b_f1, fr1, w_f2, b_f2, fr2, w_f3, b_f3, fr3, w_f4, ec=512):
    fh = w_f1.shape[1]
    ec = min(ec, e)
    t_norm = jnp.linspace(0.0, 1.0, l, dtype=F32)
    w = 2.0 * math.pi * jnp.arange(l, dtype=F32) / l
    f = jnp.linspace(1e-4, POS_BANDS - 1, POS_BANDS, dtype=F32)
    ang = w[:, None] * f[None, :]
    feat = jnp.concatenate([t_norm[:, None], jnp.cos(ang), -jnp.sin(ang)], axis=-1)
    feat_rev = jnp.concatenate([feat[:1], feat[:0:-1]], axis=0)
    feat2 = jnp.pad(jnp.concatenate([feat, feat_rev], axis=0), ((0, 0), (0, LANE - POS_EMB_DIM)))
    w1p = jnp.pad(w_f1, ((0, LANE - POS_EMB_DIM), (0, 0)))
    deltas = jnp.abs(jnp.linspace(MIN_DECAY, MAX_DECAY, e, dtype=F32))[None, :]
    nec = e // ec
    rows = l // n2
    vec = lambda a: a.reshape(1, -1)
    fixed = lambda h, o, j: (0, 0)
    return pl.pallas_call(
        _filter_kernel,
        grid=(HY_DIRS, HY_ORDER, nec),
        in_specs=[
            pl.BlockSpec((l, LANE), lambda h, o, j: (h, 0)),
            pl.BlockSpec((LANE, fh), fixed), pl.BlockSpec((1, fh), fixed), pl.BlockSpec((1, fh), fixed),
            pl.BlockSpec((fh, fh), fixed), pl.BlockSpec((1, fh), fixed), pl.BlockSpec((1, fh), fixed),
            pl.BlockSpec((fh, fh), fixed), pl.BlockSpec((1, fh), fixed), pl.BlockSpec((1, fh), fixed),
            pl.BlockSpec((fh, ec), lambda h, o, j: (0, (o * HY_DIRS + h) * nec + j)),
            pl.BlockSpec((1, ec), lambda h, o, j: (0, j)),
        ],
        out_specs=[
            pl.BlockSpec((None, n2, rows, ec), lambda h, o, j: (o, 0, h, j)),
            pl.BlockSpec((None, None, 1, ec), lambda h, o, j: (o, h, 0, j)),
        ],
        out_shape=[
            jax.ShapeDtypeStruct((HY_ORDER, n2, HY_DIRS * rows, e), BF16),
            jax.ShapeDtypeStruct((HY_ORDER, HY_DIRS, 1, e), F32),
        ],
        scratch_shapes=[pltpu.VMEM((l, fh), F32)],
        compiler_params=_cparams("arbitrary", "arbitrary", "arbitrary"),
        name="hyena_filter",
    )(feat2, w1p, vec(b_f1), vec(fr1), w_f2, vec(b_f2), vec(fr2), w_f3, vec(b_f3), vec(fr3),
      w_f4, deltas)


FFT_G = BF16_ROWS
FFT_EC = 512


def _stage_a(z_ref, ta_ref, are_ref, aim_ref, src=None):
    g = ta_ref.shape[0]
    n1 = are_ref.shape[0]
    rs = []
    for j in range(g):
        z = src[j] if src is not None else z_ref[:, j].reshape(ta_ref.shape[2], z_ref.shape[-1])
        rs.append(jnp.dot(ta_ref[j], z, preferred_element_type=F32))
    rt = _swap_major_sublane(jnp.stack(rs))
    are_ref[...] = rt[:n1].astype(BF16)
    aim_ref[...] = rt[n1:].astype(BF16)


def _fft_a_kernel(z_ref, ta_ref, are_ref, aim_ref):
    _stage_a(z_ref, ta_ref, are_ref, aim_ref)


def _fft_a(z6, comp, ta, ec=FFT_EC, g=FFT_G):
    _, p, s, n2, r, e = z6.shape
    _, m2, kk = ta.shape
    n1 = m2 // 2
    ec = min(ec, e)
    out = jax.ShapeDtypeStruct((p, n1, n2, e), BF16)
    ospec = pl.BlockSpec((None, n1, g, ec), lambda pi, j, c: (pi, 0, j, c))
    return pl.pallas_call(
        _fft_a_kernel,
        grid=(p, n2 // g, e // ec),
        in_specs=[
            pl.BlockSpec((None, None, s, g, r, ec), lambda pi, j, c: (comp, pi, 0, j, 0, c)),
            pl.BlockSpec((g, m2, kk), lambda pi, j, c: (j, 0, 0)),
        ],
        out_specs=[ospec, ospec],
        out_shape=[out, out],
        compiler_params=_cparams("parallel", "arbitrary", "arbitrary"),
        name="fft_a",
    )(z6, ta)


def _fft_bf_kernel(are_ref, aim_ref, tf_ref, ss_ref, kre_ref, kim_ref):
    g, n2, _ = are_ref.shape
    rs = lax.rsqrt(ss_ref[0] + ss_ref[1] + 1e-12)
    for j in range(g):
        a = jnp.concatenate([are_ref[j], aim_ref[j]], axis=0)
        x = jnp.dot(tf_ref[...], a, preferred_element_type=F32) * rs
        kre_ref[j] = x[:n2].astype(kre_ref.dtype)
        kim_ref[j] = x[n2:].astype(kim_ref.dtype)


def _fft_b_filter(are, aim, tf, ss, ec=FFT_EC, g=FFT_G):
    p, n1, n2, e = are.shape
    ec = min(ec, e)
    spec = pl.BlockSpec((None, g, n2, ec), lambda j, c, pi: (pi, j, 0, c))
    out = jax.ShapeDtypeStruct((p, n1, n2, e), BF16)
    return pl.pallas_call(
        _fft_bf_kernel,
        grid=(n1 // g, e // ec, p),
        in_specs=[spec, spec,
                  pl.BlockSpec((2 * n2, 2 * n2), lambda j, c, pi: (0, 0)),
                  pl.BlockSpec((None, HY_DIRS, 1, ec), lambda j, c, pi: (pi, 0, 0, c))],
        out_specs=[spec, spec],
        out_shape=[out, out],
        compiler_params=_cparams("parallel", "arbitrary", "arbitrary"),
        name="fft_b_filter",
    )(are, aim, tf, ss)


def _fft_b_kernel(are_ref, aim_ref, kre_ref, kim_ref, tf_ref, tg_ref, cre_ref, cim_ref):
    g, n2, _ = are_ref.shape
    xs = [jnp.dot(tf_ref[...], jnp.concatenate([are_ref[j], aim_ref[j]], axis=0),
                  preferred_element_type=F32) for j in range(g)]
    ys = []
    for j in range(g):
        xr, xi = xs[j][:n2], xs[j][n2:]
        kr = kre_ref[j].astype(F32)
        ki = kim_ref[j].astype(F32)
        ys.append(jnp.concatenate([xr * kr - xi * ki, xr * ki + xi * kr], axis=0).astype(BF16))
    cs = [jnp.dot(tg_ref[j], ys[j], preferred_element_type=F32) for j in range(g)]
    ct = _swap_major_sublane(jnp.stack(cs))
    cre_ref[...] = ct[:n2].astype(BF16)
    cim_ref[...] = ct[n2:].astype(BF16)


def _fft_b(are, aim, kre, kim, order, tf, tg, ec=FFT_EC, g=FFT_G):
    p, n1, n2, e = are.shape
    ec = min(ec, e)
    dspec = pl.BlockSpec((None, g, n2, ec), lambda j, c, pi: (pi, j, 0, c))
    kspec = pl.BlockSpec((None, g, n2, ec), lambda j, c, pi: (order, j, 0, c))
    ospec = pl.BlockSpec((None, n2, g, ec), lambda j, c, pi: (pi, 0, j, c))
    out = jax.ShapeDtypeStruct((p, n2, n1, e), BF16)
    return pl.pallas_call(
        _fft_b_kernel,
        grid=(n1 // g, e // ec, p),
        in_specs=[dspec, dspec, kspec, kspec,
                  pl.BlockSpec((2 * n2, 2 * n2), lambda j, c, pi: (0, 0)),
                  pl.BlockSpec((g, 2 * n2, 2 * n2), lambda j, c, pi: (j, 0, 0))],
        out_specs=[ospec, ospec],
        out_shape=[out, out],
        compiler_params=_cparams("parallel", "arbitrary", "arbitrary"),
        name="fft_b",
    )(are, aim, kre, kim, tf, tg)


def _stage_c(cre_ref, cim_ref, tc_ref, j):
    c = jnp.concatenate([cre_ref[j], cim_ref[j]], axis=0)
    return jnp.dot(tc_ref[...], c, preferred_element_type=F32)


def _rows(ref, j):
    two, _, h1, ec = ref.shape
    return ref[:, j].reshape(two * h1, ec)


def _fft_c1_kernel(cre_ref, cim_ref, tc_ref, z_ref, gate_ref, hb_ref, ta_ref,
                   z1_ref, are_ref, aim_ref):
    g = cre_ref.shape[0]
    two, _, h1, ec = z_ref.shape
    ys = [_stage_c(cre_ref, cim_ref, tc_ref, j) for j in range(g)]
    z1s = []
    for j in range(g):
        z1 = _rows(gate_ref, j).astype(F32) * (ys[j] + hb_ref[...] * _rows(z_ref, j).astype(F32))
        z1 = z1.astype(BF16)
        z1_ref[:, j] = z1.reshape(two, h1, ec)
        z1s.append(z1)
    _stage_a(None, ta_ref, are_ref, aim_ref, src=z1s)


def _fft_c2_kernel(cre_ref, cim_ref, tc_ref, z_ref, gate_ref, hb_ref, sg_ref, y_ref):
    g = cre_ref.shape[0]
    two, _, h1, ec = z_ref.shape
    ys = [_stage_c(cre_ref, cim_ref, tc_ref, j) for j in range(g)]
    for j in range(g):
        z2 = _rows(gate_ref, j).astype(F32) * (ys[j] + hb_ref[...] * _rows(z_ref, j).astype(F32))
        sg = _rows(sg_ref, j).astype(F32)
        out = z2 * (sg / (1.0 + jnp.exp(-sg)))
        y_ref[:, j] = out.astype(BF16).reshape(two, h1, ec)


def _fft_c1(cre, cim, tc, u6, z_comp, gate_comp, hb, ta, ec=FFT_EC, g=FFT_G):
    p, n2, n1, e = cre.shape
    _, _, two, _, h1, _ = u6.shape
    ec = min(ec, e)
    cspec = pl.BlockSpec((None, g, n1, ec), lambda pi, j, c: (pi, j, 0, c))
    zspec = lambda comp: pl.BlockSpec((None, None, two, g, h1, ec),
                                      lambda pi, j, c: (comp, pi, 0, j, 0, c))
    aspec = pl.BlockSpec((None, n1, g, ec), lambda pi, j, c: (pi, 0, j, c))
    aout = jax.ShapeDtypeStruct((p, n1, n2, e), BF16)
    return pl.pallas_call(
        _fft_c1_kernel,
        grid=(p, n2 // g, e // ec),
        in_specs=[cspec, cspec,
                  pl.BlockSpec(tc.shape, lambda pi, j, c: (0, 0)),
                  zspec(z_comp), zspec(gate_comp),
                  pl.BlockSpec((1, ec), lambda pi, j, c: (0, c)),
                  pl.BlockSpec((g,) + ta.shape[1:], lambda pi, j, c: (j, 0, 0))],
        out_specs=[pl.BlockSpec((None, two, g, h1, ec), lambda pi, j, c: (pi, 0, j, 0, c)),
                   aspec, aspec],
        out_shape=[jax.ShapeDtypeStruct((p, two, n2, h1, e), BF16), aout, aout],
        compiler_params=_cparams("parallel", "arbitrary", "arbitrary"),
        name="fft_c1",
    )(cre, cim, tc, u6, u6, hb, ta)


def _fft_c2(cre, cim, tc, z1, u6, gate_comp, p6, sg_comp, hb, ec=FFT_EC, g=FFT_G):
    p, n2, n1, e = cre.shape
    _, _, two, _, h1, _ = u6.shape
    ec = min(ec, e)
    cspec = pl.BlockSpec((None, g, n1, ec), lambda pi, j, c: (pi, j, 0, c))
    zspec = lambda comp: pl.BlockSpec((None, None, two, g, h1, ec),
                                      lambda pi, j, c: (comp, pi, 0, j, 0, c))
    z1spec = pl.BlockSpec((None, two, g, h1, ec), lambda pi, j, c: (pi, 0, j, 0, c))
    return pl.pallas_call(
        _fft_c2_kernel,
        grid=(p, n2 // g, e // ec),
        in_specs=[cspec, cspec,
                  pl.BlockSpec(tc.shape, lambda pi, j, c: (0, 0)),
                  z1spec, zspec(gate_comp),
                  pl.BlockSpec((1, ec), lambda pi, j, c: (0, c)),
                  zspec(sg_comp)],
        out_specs=z1spec,
        out_shape=jax.ShapeDtypeStruct((p, two, n2, h1, e), BF16),
        compiler_params=_cparams("parallel", "arbitrary", "arbitrary"),
        name="fft_c2",
    )(cre, cim, tc, z1, u6, hb, p6)


def _attn_kernel(slope_ref, sink_ref, q_ref, kp_ref, kc_ref, kn_ref, vp_ref, vc_ref, vn_ref,
                 glo_ref, ghi_ref, o_ref, *, group):
    n = pl.program_id(1)
    nb = pl.num_programs(1)
    blk = q_ref.shape[0]
    hd = HEAD_DIM
    n_kv = kc_ref.shape[1] // hd
    heads_per_g = glo_ref.shape[1] // hd
    nk = 3 * blk
    sj = lax.broadcasted_iota(jnp.int32, (nk, blk), 0)
    qi = lax.broadcasted_iota(jnp.int32, (nk, blk), 1)
    dist = jnp.abs(qi - sj + blk)
    valid = (dist <= WINDOW) & ((sj >= blk) | (n > 0)) & ((sj < 2 * blk) | (n < nb - 1))
    distm = jnp.where(valid, dist.astype(F32), MASK_DIST)
    pad = BF16_ROWS
    first_q = lax.broadcasted_iota(jnp.int32, (pad, blk), 0) == 0
    first_d = lax.broadcasted_iota(jnp.int32, (pad, hd), 0) == 0
    ones_blk = jnp.ones((nk, hd), BF16)
    sink_v = jnp.concatenate([jnp.zeros((pad, hd), BF16),
                              jnp.where(first_d, 1.0, 0.0).astype(BF16)], axis=1)
    for kvh in range(n_kv):
        ksl = slice(kvh * hd, (kvh + 1) * hd)
        k = jnp.concatenate([kp_ref[:, ksl], kc_ref[:, ksl], kn_ref[:, ksl]], axis=0)
        v = jnp.concatenate([vp_ref[:, ksl], vc_ref[:, ksl], vn_ref[:, ksl]], axis=0)
        v_aug = jnp.concatenate([jnp.concatenate([v, ones_blk], axis=1), sink_v], axis=0)
        heads = range(kvh * group, (kvh + 1) * group)
        q = jnp.concatenate([q_ref[:, h * hd:(h + 1) * hd] for h in heads], axis=0)
        st = lax.dot_general(k, q, (((1,), (1,)), ((), ())), preferred_element_type=F32)
        for gi, h in enumerate(heads):
            logit = st[:, gi * blk:(gi + 1) * blk] - slope_ref[h] * distm
            sk = sink_ref[h]
            m = jnp.maximum(jnp.max(logit, axis=0, keepdims=True), sk)
            p = jnp.exp2(logit - m)
            sink_p = jnp.where(first_q, jnp.exp2(sk - m), 0.0)
            pt = jnp.concatenate([p, sink_p], axis=0).astype(BF16)
            oa = lax.dot_general(pt, v_aug, (((0,), (0,)), ((), ())),
                                 preferred_element_type=F32)
            o = oa[:, :hd] * (1.0 / oa[:, hd:])
            g_half = glo_ref if h < heads_per_g else ghi_ref
            hg = h % heads_per_g
            gg = g_half[:, hg * hd:(hg + 1) * hd].astype(F32)
            o_ref[:, h * hd:(h + 1) * hd] = (o * (gg / (1.0 + jnp.exp(-gg)))).astype(o_ref.dtype)


def _attention(proj, slopes, sink, n_heads):
    bsz, l, _ = proj.shape
    hd, blk = HEAD_DIM, WINDOW
    group = n_heads // N_KV_HEADS
    aw = n_heads * hd
    kvw = N_KV_HEADS * hd
    nb = l // blk
    k0 = aw // kvw
    v0 = k0 + 1
    gw = aw // 2
    g0 = (aw + 2 * kvw) // gw
    smem = pl.BlockSpec(memory_space=pltpu.SMEM)
    prev = lambda n: jnp.maximum(n - 1, 0)
    nxt = lambda n: jnp.minimum(n + 1, nb - 1)
    same = lambda n: n
    kv = lambda c0, f: pl.BlockSpec((None, blk, kvw), lambda b, n: (b, f(n), c0))
    gspec = lambda c0: pl.BlockSpec((None, blk, gw), lambda b, n: (b, n, c0))
    assert (aw + 2 * kvw) % gw == 0 and aw % kvw == 0
    return pl.pallas_call(
        functools.partial(_attn_kernel, group=group),
        grid=(bsz, nb),
        in_specs=[smem, smem,
                  pl.BlockSpec((None, blk, aw), lambda b, n: (b, n, 0)),
                  kv(k0, prev), kv(k0, same), kv(k0, nxt),
                  kv(v0, prev), kv(v0, same), kv(v0, nxt),
                  gspec(g0), gspec(g0 + 1)],
        out_specs=pl.BlockSpec((None, blk, aw), lambda b, n: (b, n, 0)),
        out_shape=jax.ShapeDtypeStruct((bsz, l, aw), BF16),
        compiler_params=_cparams("parallel", "arbitrary"),
        name="swa_attention",
    )(slopes, sink, proj, proj, proj, proj, proj, proj, proj, proj, proj)


def _hyena_filter_spectrum(l, e, hy, tabs):
    _, ta_filt, tb_fwd, _, _ = tabs
    n1 = FFT_N1
    n2 = 2 * l // n1
    k_un, ss = _filters(l, e, n2, hy["w_f1"], hy["b_f1"], hy["fr1"], hy["w_f2"], hy["b_f2"],
                        hy["fr2"], hy["w_f3"], hy["b_f3"], hy["fr3"], hy["w_f4"])
    k6 = k_un.reshape(1, HY_ORDER, 1, n2, n1, e)
    are, aim = _fft_a(k6, 0, ta_filt)
    return _fft_b_filter(are, aim, tb_fwd, ss)


def _hyena_layer(x, hy, kf, tabs, ln_g, ln_b, alpha):
    bsz, l, d = x.shape
    e = hy["w_out"].shape[0]
    ta_data, _, tb_fwd, tb_inv, tc = tabs
    kre, kim = kf
    n1 = FFT_N1
    n2 = 2 * l // n1
    h1 = n1 // 2
    proj = _inproj(x.reshape(bsz * l, d), hy["w_in"], hy["b_in"], 4, permute=(l, n2))
    u = _shortconv(proj, hy["w_sc"], hy["b_sc"])
    pair = lambda a: a.reshape(a.shape[0], bsz // 2, 2, n2, h1, e)
    u6, p6 = pair(u), pair(proj)
    are, aim = _fft_a(u6, 0, ta_data)
    cre, cim = _fft_b(are, aim, kre, kim, 0, tb_fwd, tb_inv)
    z1, are, aim = _fft_c1(cre, cim, tc, u6, 0, 1, hy["h_bias"][0:1], ta_data)
    cre, cim = _fft_b(are, aim, kre, kim, 1, tb_fwd, tb_inv)
    y = _fft_c2(cre, cim, tc, z1, u6, 2, p6, 3, hy["h_bias"][1:2])
    out = _outproj_ln_permuted(y.reshape(bsz, n2, h1, e), hy["w_out"], hy["b_out"],
                               x.reshape(bsz, h1, n2, d), ln_g, ln_b, alpha)
    return out.reshape(bsz, l, d)


def _attention_layer(x, at, ln_g, ln_b, alpha):
    bsz, l, d = x.shape
    n_heads = at["sink"].shape[0]
    x2 = x.reshape(bsz * l, d)
    proj = _inproj(x2, at["w_in"], at["b_in"], 1, col_scale=at["col_scale"])
    proj = proj.reshape(bsz, l, -1)
    o = _attention(proj, at["slopes"], at["sink"] * LOG2E, n_heads)
    out = _outproj_ln(o.reshape(bsz * l, -1), at["w_out"], at["b_out"], x2, ln_g, ln_b, alpha)
    return out.reshape(bsz, l, d)


def kernel(x_prompt, x_sample, ln_g, ln_b, hy_w_in, hy_b_in, hy_w_sc, hy_b_sc, hy_w_f1, hy_b_f1,
           hy_fr1, hy_w_f2, hy_b_f2, hy_fr2, hy_w_f3, hy_b_f3, hy_fr3, hy_w_f4, hy_h_bias,
           hy_w_out, hy_b_out, at_w_in, at_sink, at_w_out):
    depth = ln_g.shape[0]
    alpha = (2 * depth) ** 0.25
    l = x_prompt.shape[1]
    d = x_prompt.shape[2]
    assert x_sample.shape[1] == l and (2 * l) % FFT_N1 == 0
    assert x_prompt.shape[0] % 2 == 0 and x_sample.shape[0] % 2 == 0
    tabs = tuple(jnp.asarray(t).astype(BF16) for t in _dft_tables(2 * l))
    row = lambda a: a.reshape(1, -1)

    layers = []
    for i in range(depth):
        j = i // N_MIXERS
        if i % N_MIXERS == 0:
            e = hy_w_out.shape[1]
            hy = dict(w_in=hy_w_in[j].astype(BF16), b_in=row(hy_b_in[j]), w_sc=hy_w_sc[j],
                      b_sc=row(hy_b_sc[j]), w_f1=hy_w_f1[j], b_f1=hy_b_f1[j], fr1=hy_fr1[j],
                      w_f2=hy_w_f2[j], b_f2=hy_b_f2[j], fr2=hy_fr2[j], w_f3=hy_w_f3[j],
                      b_f3=hy_b_f3[j], fr3=hy_fr3[j], w_f4=hy_w_f4[j], h_bias=hy_h_bias[j],
                      w_out=hy_w_out[j].astype(BF16), b_out=row(hy_b_out[j]))
            kf = _hyena_filter_spectrum(l, e, hy, tabs)
            layers.append(("hyena", hy, kf))
        else:
            n_heads = at_sink.shape[1]
            aw = n_heads * HEAD_DIM
            width = at_w_in.shape[2]
            col_scale = jnp.where(jnp.arange(width) < aw, HEAD_DIM ** -0.5 * LOG2E, 1.0)
            slopes = jnp.exp2(-8.0 * jnp.arange(1, n_heads + 1, dtype=F32) / n_heads) * LOG2E
            at = dict(w_in=at_w_in[j].astype(BF16), b_in=jnp.zeros((1, width), F32),
                      col_scale=row(col_scale.astype(F32)), slopes=slopes,
                      sink=at_sink[j], w_out=at_w_out[j].astype(BF16),
                      b_out=jnp.zeros((1, d), F32))
            layers.append(("attn", at, None))

    def trunk(x):
        for i, (kind, prm, kf) in enumerate(layers):
            g, b = row(ln_g[i]), row(ln_b[i])
            if kind == "hyena":
                x = _hyena_layer(x, prm, kf, tabs, g, b, alpha)
            else:
                x = _attention_layer(x, prm, g, b, alpha)
        return x

    return (trunk(x_prompt), trunk(x_sample))
```

```python
import functools
import math

import numpy as np
import jax
import jax.numpy as jnp
from jax import lax
from jax.experimental import pallas as pl
from jax.experimental.pallas import tpu as pltpu

F32 = jnp.float32
BF16 = jnp.bfloat16

N_MIXERS = 2
HY_ORDER = 2
HY_DIRS = 2
SHORT_CONV = 3
POS_EMB_DIM = 33
POS_BANDS = (POS_EMB_DIM - 1) // 2
DECAY_TARGET = 1e-2
FAST_DECAY_PCT = 0.3
SLOW_DECAY_PCT = 1.5
MIN_DECAY = math.log(DECAY_TARGET) / SLOW_DECAY_PCT
MAX_DECAY = math.log(DECAY_TARGET) / FAST_DECAY_PCT
N_KV_HEADS = 4
HEAD_DIM = 128
WINDOW = 128
LN_EPS = 1e-5
LOG2E = 1.4426950408889634
MASK_DIST = 1e30

FFT_N1 = 128
LANE = 128
BF16_ROWS = 16
F32_ROWS = 8
VMEM_LIMIT = 56 * 1024 * 1024


def _cparams(*sem):
    return pltpu.CompilerParams(dimension_semantics=sem, vmem_limit_bytes=VMEM_LIMIT)


def _swap_major_sublane(x):
    return jnp.swapaxes(x, 0, 1)


def _silu(x):
    return x * (0.5 * jnp.tanh(0.5 * x) + 0.5)


def _block(m):
    return np.block([[m.real, -m.imag], [m.imag, m.real]])


@functools.lru_cache(maxsize=None)
def _dft_tables(n):
    n1, n2 = FFT_N1, n // FFT_N1
    h1 = n1 // 2
    k1 = np.arange(n1)[:, None]
    ta_data, ta_filt = [], []
    for j in range(n2):
        m = np.exp(-2j * np.pi * (j * k1 / n + np.arange(n1)[None, :] * k1 / n1))
        ta_data.append(_block(m[:, :h1]))
        ta_filt.append(np.concatenate([m.real, m.imag], 0))
    f2 = np.exp(-2j * np.pi * np.arange(n2)[:, None] * np.arange(n2)[None, :] / n2)
    tb_fwd = _block(f2)
    t1 = np.arange(n2)[:, None]
    tb_inv = [_block(np.exp(2j * np.pi * (kk * t1 / n + np.arange(n2)[None, :] * t1 / n2)))
              for kk in range(n1)]
    hc = np.exp(2j * np.pi * np.arange(h1)[:, None] * np.arange(n1)[None, :] / n1) / n
    tc = _block(hc)
    f = lambda a: np.asarray(a, np.float32)
    return f(np.stack(ta_data)), f(np.stack(ta_filt)), f(tb_fwd), f(np.stack(tb_inv)), f(tc)


def _inproj_kernel(x_ref, w_ref, b_ref, cs_ref, o_ref, xb_ref, *, scaled):
    @pl.when(pl.program_id(1) == 0)
    def _():
        xb_ref[...] = x_ref[...].astype(BF16)

    acc = jnp.dot(xb_ref[...], w_ref[...], preferred_element_type=F32)
    if scaled:
        acc = acc * cs_ref[...]
    val = (acc + b_ref[...]).astype(o_ref.dtype)
    if len(o_ref.shape) == 3:
        n2, n1b, tn = o_ref.shape
        val = _swap_major_sublane(val.reshape(n1b, n2, tn))
    o_ref[...] = val


def _inproj(x, w, b, n_comp, col_scale=None, permute=None, tm=1024, tn=1024):
    m, k = x.shape
    n = w.shape[1]
    wc = n // n_comp
    tn = min(tn, wc)
    ncb = wc // tn
    scaled = col_scale is not None
    if not scaled:
        col_scale = b
    if permute is None:
        out_spec = pl.BlockSpec((None, tm, tn), lambda i, j: (j // ncb, i, j % ncb))
        out_shape = jax.ShapeDtypeStruct((n_comp, m, wc), BF16)
    else:
        l, n2 = permute
        tm = BF16_ROWS * n2
        tpb = l // tm
        out_spec = pl.BlockSpec((None, None, n2, BF16_ROWS, tn),
                                lambda i, j: (j // ncb, i // tpb, 0, i % tpb, j % ncb))
        out_shape = jax.ShapeDtypeStruct((n_comp, m // l, n2, l // n2, wc), BF16)
    vec = pl.BlockSpec((1, tn), lambda i, j: (0, j))
    return pl.pallas_call(
        functools.partial(_inproj_kernel, scaled=scaled),
        grid=(m // tm, n // tn),
        in_specs=[
            pl.BlockSpec((tm, k), lambda i, j: (i, 0)),
            pl.BlockSpec((k, tn), lambda i, j: (0, j)),
            vec, vec,
        ],
        out_specs=out_spec,
        out_shape=out_shape,
        scratch_shapes=[pltpu.VMEM((tm, k), BF16)],
        compiler_params=_cparams("parallel", "arbitrary"),
        name="inproj",
    )(x, w, b, col_scale)


def _outproj_ln_kernel(y_ref, w_ref, b_ref, x_ref, g_ref, beta_ref, o_ref, *, alpha, permuted):
    y = y_ref[...]
    if permuted:
        nb, na, kk = y.shape
        y = y.reshape(nb * na, kk)
    h = jnp.dot(y, w_ref[...], preferred_element_type=F32) + b_ref[...]
    if permuted:
        h = _swap_major_sublane(h.reshape(nb, na, h.shape[-1]))
    r = alpha * x_ref[...] + h
    mu = jnp.mean(r, axis=-1, keepdims=True)
    d = r - mu
    var = jnp.mean(d * d, axis=-1, keepdims=True)
    o_ref[...] = d * lax.rsqrt(var + LN_EPS) * g_ref[...] + beta_ref[...]


def _outproj_ln(y, w, b, x, g, beta, alpha, tm=512):
    m, k = y.shape
    d = w.shape[1]
    row = lambda i: (i, 0)
    fixed = lambda i: (0, 0)
    return pl.pallas_call(
        functools.partial(_outproj_ln_kernel, alpha=alpha, permuted=False),
        grid=(m // tm,),
        in_specs=[
            pl.BlockSpec((tm, k), row),
            pl.BlockSpec((k, d), fixed),
            pl.BlockSpec((1, d), fixed),
            pl.BlockSpec((tm, d), row),
            pl.BlockSpec((1, d), fixed),
            pl.BlockSpec((1, d), fixed),
        ],
        out_specs=pl.BlockSpec((tm, d), row),
        out_shape=jax.ShapeDtypeStruct((m, d), F32),
        compiler_params=_cparams("parallel"),
        name="outproj_ln",
    )(y, w, b, x, g, beta)


def _outproj_ln_permuted(y, w, b, x, g, beta, alpha):
    bsz, n2, h1, k = y.shape
    d = w.shape[1]
    nb = F32_ROWS
    fixed = lambda bi, j: (0, 0)
    xspec = pl.BlockSpec((None, h1, nb, d), lambda bi, j: (bi, 0, j, 0))
    return pl.pallas_call(
        functools.partial(_outproj_ln_kernel, alpha=alpha, permuted=True),
        grid=(bsz, n2 // nb),
        in_specs=[
            pl.BlockSpec((None, nb, h1, k), lambda bi, j: (bi, j, 0, 0)),
            pl.BlockSpec((k, d), fixed),
            pl.BlockSpec((1, d), fixed),
            xspec,
            pl.BlockSpec((1, d), fixed),
            pl.BlockSpec((1, d), fixed),
        ],
        out_specs=xspec,
        out_shape=jax.ShapeDtypeStruct((bsz, h1, n2, d), F32),
        compiler_params=_cparams("parallel", "parallel"),
        name="outproj_ln_perm",
    )(y, w, b, x, g, beta)


def _shortconv_kernel(cur_ref, prev_ref, next_ref, w_ref, b_ref, o_ref):
    i = pl.program_id(2)
    last = pl.num_programs(2) - 1
    n2, n1b, tc = cur_ref.shape
    cur = cur_ref[...].astype(F32)
    row = lax.broadcasted_iota(jnp.int32, (n1b, tc), 0)
    prev_row = jnp.where(i > 0, prev_ref[0, n1b - 1:n1b, :].astype(F32), 0.0)
    lo = jnp.where(row == 0, prev_row, pltpu.roll(cur[n2 - 1], 1, axis=0))
    next_row = jnp.where(i < last, next_ref[0, 0:1, :].astype(F32), 0.0)
    hi = jnp.where(row == n1b - 1, next_row, pltpu.roll(cur[0], n1b - 1, axis=0))
    down = jnp.concatenate([lo[None], cur[:n2 - 1]], axis=0)
    up = jnp.concatenate([cur[1:], hi[None]], axis=0)
    out = b_ref[...] + down * w_ref[0:1, :] + cur * w_ref[1:2, :] + up * w_ref[2:3, :]
    o_ref[...] = out.astype(o_ref.dtype)


def _shortconv(proj, w_sc, b_sc, tc=1024):
    _, bsz, n2, h1, e = proj.shape
    ncomp = w_sc.shape[1] // e
    n1b = BF16_ROWS
    tc = min(tc, e)
    ncb = e // tc
    nblk = h1 // n1b
    wcol = lambda c, b, i, j: (0, c * ncb + j)
    blk = lambda c, b, i, j: (c, b, 0, i, j)
    return pl.pallas_call(
        _shortconv_kernel,
        grid=(ncomp, bsz, nblk, ncb),
        in_specs=[
            pl.BlockSpec((None, None, n2, n1b, tc), blk),
            pl.BlockSpec((None, None, 1, n1b, tc),
                         lambda c, b, i, j: (c, b, n2 - 1, jnp.maximum(i - 1, 0), j)),
            pl.BlockSpec((None, None, 1, n1b, tc),
                         lambda c, b, i, j: (c, b, 0, jnp.minimum(i + 1, nblk - 1), j)),
            pl.BlockSpec((SHORT_CONV, tc), wcol),
            pl.BlockSpec((1, tc), wcol),
        ],
        out_specs=pl.BlockSpec((None, None, n2, n1b, tc), blk),
        out_shape=jax.ShapeDtypeStruct((ncomp, bsz, n2, h1, e), BF16),
        compiler_params=_cparams("parallel", "parallel", "arbitrary", "arbitrary"),
        name="shortconv",
    )(proj, proj, proj, w_sc, b_sc)


def _filter_kernel(feat_ref, w1_ref, b1_ref, fr1_ref, w2_ref, b2_ref, fr2_ref, w3_ref, b3_ref,
                   fr3_ref, w4_ref, delta_ref, k_ref, ss_ref, h_ref):
    half = pl.program_id(0)
    hp = lax.Precision.HIGHEST

    @pl.when((pl.program_id(1) == 0) & (pl.program_id(2) == 0))
    def _():
        h = jnp.sin(fr1_ref[...] * (jnp.dot(feat_ref[...], w1_ref[...], precision=hp,
                                             preferred_element_type=F32) + b1_ref[...]))
        h = jnp.sin(fr2_ref[...] * (jnp.dot(h, w2_ref[...], precision=hp,
                                             preferred_element_type=F32) + b2_ref[...]))
        h = jnp.sin(fr3_ref[...] * (jnp.dot(h, w3_ref[...], precision=hp,
                                             preferred_element_type=F32) + b3_ref[...]))
        h_hi = h.astype(BF16)
        h_lo = (h - h_hi.astype(F32)).astype(BF16)
        h_ref[...] = jnp.concatenate([h_hi, h_hi, h_lo], axis=1)

    w4 = w4_ref[...]
    w_hi = w4.astype(BF16)
    w_lo = (w4 - w_hi.astype(F32)).astype(BF16)
    k = jnp.dot(h_ref[...], jnp.concatenate([w_hi, w_lo, w_hi], axis=0), preferred_element_type=F32)
    t_norm = feat_ref[:, 0:1]
    k = k * jnp.exp(-t_norm * delta_ref[...])
    row = lax.broadcasted_iota(jnp.int32, k.shape, 0)
    k = jnp.where((row == 0) & (half == 1), 0.0, k)
    ss_ref[...] = jnp.sum(k * k, axis=0, keepdims=True)
    n2, rows, ec = k_ref.shape
    k_ref[...] = _swap_major_sublane(k.astype(k_ref.dtype).reshape(rows, n2, ec))


def _filters(l, e, n2, w_f1, b_f1, fr1, w_f2, b_f2, fr2, w_f3, b_f3, fr3, w_f4, ec=512):
    fh = w_f1.shape[1]
    ec = min(ec, e)
    t_norm = jnp.linspace(0.0, 1.0, l, dtype=F32)
    w = 2.0 * math.pi * jnp.arange(l, dtype=F32) / l
    f = jnp.linspace(1e-4, POS_BANDS - 1, POS_BANDS, dtype=F32)
    ang = w[:, None] * f[None, :]
    feat = jnp.concatenate([t_norm[:, None], jnp.cos(ang), -jnp.sin(ang)], axis=-1)
    feat_rev = jnp.concatenate([feat[:1], feat[:0:-1]], axis=0)
    feat2 = jnp.pad(jnp.concatenate([feat, feat_rev], axis=0), ((0, 0), (0, LANE - POS_EMB_DIM)))
    w1p = jnp.pad(w_f1, ((0, LANE - POS_EMB_DIM), (0, 0)))
    deltas = jnp.abs(jnp.linspace(MIN_DECAY, MAX_DECAY, e, dtype=F32))[None, :]
    nec = e // ec
    rows = l // n2
    vec = lambda a: a.reshape(1, -1)
    fixed = lambda h, o, j: (0, 0)
    return pl.pallas_call(
        _filter_kernel,
        grid=(HY_DIRS, HY_ORDER, nec),
        in_specs=[
            pl.BlockSpec((l, LANE), lambda h, o, j: (h, 0)),
            pl.BlockSpec((LANE, fh), fixed), pl.BlockSpec((1, fh), fixed), pl.BlockSpec((1, fh), fixed),
            pl.BlockSpec((fh, fh), fixed), pl.BlockSpec((1, fh), fixed), pl.BlockSpec((1, fh), fixed),
            pl.BlockSpec((fh, fh), fixed), pl.BlockSpec((1, fh), fixed), pl.BlockSpec((1, fh), fixed),
            pl.BlockSpec((fh, ec), lambda h, o, j: (0, (o * HY_DIRS + h) * nec + j)),
            pl.BlockSpec((1, ec), lambda h, o, j: (0, j)),
        ],
        out_specs=[
            pl.BlockSpec((None, n2, rows, ec), lambda h, o, j: (o, 0, h, j)),
            pl.BlockSpec((None, None, 1, ec), lambda h, o, j: (o, h, 0, j)),
        ],
        out_shape=[
            jax.ShapeDtypeStruct((HY_ORDER, n2, HY_DIRS * rows, e), BF16),
            jax.ShapeDtypeStruct((HY_ORDER, HY_DIRS, 1, e), F32),
        ],
        scratch_shapes=[pltpu.VMEM((l, 3 * fh), BF16)],
        compiler_params=_cparams("arbitrary", "arbitrary", "arbitrary"),
        name="hyena_filter",
    )(feat2, w1p, vec(b_f1), vec(fr1), w_f2, vec(b_f2), vec(fr2), w_f3, vec(b_f3), vec(fr3),
      w_f4, deltas)


FFT_G = BF16_ROWS
FFT_EC = 512


def _stage_a(z_ref, ta_ref, are_ref, aim_ref, src=None):
    g = ta_ref.shape[0]
    n1 = are_ref.shape[0]
    rs = []
    for j in range(g):
        z = src[j] if src is not None else z_ref[:, j].reshape(ta_ref.shape[2], z_ref.shape[-1])
        rs.append(jnp.dot(ta_ref[j], z, preferred_element_type=F32).astype(BF16))
    rt = _swap_major_sublane(jnp.stack(rs))
    are_ref[...] = rt[:n1]
    aim_ref[...] = rt[n1:]


def _fft_a_kernel(z_ref, ta_ref, are_ref, aim_ref):
    _stage_a(z_ref, ta_ref, are_ref, aim_ref)


def _fft_a(z6, comp, ta, ec=FFT_EC, g=FFT_G):
    _, p, s, n2, r, e = z6.shape
    _, m2, kk = ta.shape
    n1 = m2 // 2
    ec = min(ec, e)
    out = jax.ShapeDtypeStruct((p, n1, n2, e), BF16)
    ospec = pl.BlockSpec((None, n1, g, ec), lambda pi, j, c: (pi, 0, j, c))
    return pl.pallas_call(
        _fft_a_kernel,
        grid=(p, n2 // g, e // ec),
        in_specs=[
            pl.BlockSpec((None, None, s, g, r, ec), lambda pi, j, c: (comp, pi, 0, j, 0, c)),
            pl.BlockSpec((g, m2, kk), lambda pi, j, c: (j, 0, 0)),
        ],
        out_specs=[ospec, ospec],
        out_shape=[out, out],
        compiler_params=_cparams("parallel", "arbitrary", "arbitrary"),
        name="fft_a",
    )(z6, ta)


def _fft_bf_kernel(are_ref, aim_ref, tf_ref, ss_ref, kre_ref, kim_ref):
    g, n2, _ = are_ref.shape
    rs = lax.rsqrt(ss_ref[0] + ss_ref[1] + 1e-12)
    for j in range(g):
        a = jnp.concatenate([are_ref[j], aim_ref[j]], axis=0)
        x = jnp.dot(tf_ref[...], a, preferred_element_type=F32) * rs
        kre_ref[j] = x[:n2].astype(kre_ref.dtype)
        kim_ref[j] = x[n2:].astype(kim_ref.dtype)


def _fft_b_filter(are, aim, tf, ss, ec=FFT_EC, g=FFT_G):
    p, n1, n2, e = are.shape
    ec = min(ec, e)
    spec = pl.BlockSpec((None, g, n2, ec), lambda j, c, pi: (pi, j, 0, c))
    out = jax.ShapeDtypeStruct((p, n1, n2, e), BF16)
    return pl.pallas_call(
        _fft_bf_kernel,
        grid=(n1 // g, e // ec, p),
        in_specs=[spec, spec,
                  pl.BlockSpec((2 * n2, 2 * n2), lambda j, c, pi: (0, 0)),
                  pl.BlockSpec((None, HY_DIRS, 1, ec), lambda j, c, pi: (pi, 0, 0, c))],
        out_specs=[spec, spec],
        out_shape=[out, out],
        compiler_params=_cparams("parallel", "arbitrary", "arbitrary"),
        name="fft_b_filter",
    )(are, aim, tf, ss)


def _fft_b_kernel(are_ref, aim_ref, kre_ref, kim_ref, tf_ref, tg_ref, cre_ref, cim_ref):
    g, n2, _ = are_ref.shape
    xs = [jnp.dot(tf_ref[...], jnp.concatenate([are_ref[j], aim_ref[j]], axis=0),
                  preferred_element_type=F32) for j in range(g)]
    ys = []
    for j in range(g):
        xr, xi = xs[j][:n2], xs[j][n2:]
        kr = kre_ref[j].astype(F32)
        ki = kim_ref[j].astype(F32)
        ys.append(jnp.concatenate([xr * kr - xi * ki, xr * ki + xi * kr], axis=0).astype(BF16))
    cs = [jnp.dot(tg_ref[j], ys[j], preferred_element_type=F32).astype(BF16) for j in range(g)]
    ct = _swap_major_sublane(jnp.stack(cs))
    cre_ref[...] = ct[:n2]
    cim_ref[...] = ct[n2:]


def _fft_b(are, aim, kre, kim, order, tf, tg, ec=FFT_EC, g=FFT_G):
    p, n1, n2, e = are.shape
    ec = min(ec, e)
    dspec = pl.BlockSpec((None, g, n2, ec), lambda j, c, pi: (pi, j, 0, c))
    kspec = pl.BlockSpec((None, g, n2, ec), lambda j, c, pi: (order, j, 0, c))
    ospec = pl.BlockSpec((None, n2, g, ec), lambda j, c, pi: (pi, 0, j, c))
    out = jax.ShapeDtypeStruct((p, n2, n1, e), BF16)
    return pl.pallas_call(
        _fft_b_kernel,
        grid=(n1 // g, e // ec, p),
        in_specs=[dspec, dspec, kspec, kspec,
                  pl.BlockSpec((2 * n2, 2 * n2), lambda j, c, pi: (0, 0)),
                  pl.BlockSpec((g, 2 * n2, 2 * n2), lambda j, c, pi: (j, 0, 0))],
        out_specs=[ospec, ospec],
        out_shape=[out, out],
        compiler_params=_cparams("parallel", "arbitrary", "arbitrary"),
        name="fft_b",
    )(are, aim, kre, kim, tf, tg)


def _stage_c(cre_ref, cim_ref, tc_ref, j):
    c = jnp.concatenate([cre_ref[j], cim_ref[j]], axis=0)
    return jnp.dot(tc_ref[...], c, preferred_element_type=F32)


def _rows(ref, j):
    two, _, h1, ec = ref.shape
    return ref[:, j].reshape(two * h1, ec)


def _fft_c1_kernel(cre_ref, cim_ref, tc_ref, z_ref, gate_ref, hb_ref, ta_ref,
                   z1_ref, are_ref, aim_ref):
    g = cre_ref.shape[0]
    two, _, h1, ec = z_ref.shape
    ys = [_stage_c(cre_ref, cim_ref, tc_ref, j) for j in range(g)]
    z1s = []
    for j in range(g):
        z1 = _rows(gate_ref, j).astype(F32) * (ys[j] + hb_ref[...] * _rows(z_ref, j).astype(F32))
        z1 = z1.astype(BF16)
        z1_ref[:, j] = z1.reshape(two, h1, ec)
        z1s.append(z1)
    _stage_a(None, ta_ref, are_ref, aim_ref, src=z1s)


def _fft_c2_kernel(cre_ref, cim_ref, tc_ref, z_ref, gate_ref, hb_ref, sg_ref, y_ref):
    g = cre_ref.shape[0]
    two, _, h1, ec = z_ref.shape
    ys = [_stage_c(cre_ref, cim_ref, tc_ref, j) for j in range(g)]
    for j in range(g):
        z2 = _rows(gate_ref, j).astype(F32) * (ys[j] + hb_ref[...] * _rows(z_ref, j).astype(F32))
        out = z2 * _silu(_rows(sg_ref, j).astype(F32))
        y_ref[:, j] = out.astype(BF16).reshape(two, h1, ec)


def _fft_c1(cre, cim, tc, u6, z_comp, gate_comp, hb, ta, ec=FFT_EC, g=FFT_G):
    p, n2, n1, e = cre.shape
    _, _, two, _, h1, _ = u6.shape
    ec = min(ec, e)
    cspec = pl.BlockSpec((None, g, n1, ec), lambda pi, j, c: (pi, j, 0, c))
    zspec = lambda comp: pl.BlockSpec((None, None, two, g, h1, ec),
                                      lambda pi, j, c: (comp, pi, 0, j, 0, c))
    aspec = pl.BlockSpec((None, n1, g, ec), lambda pi, j, c: (pi, 0, j, c))
    aout = jax.ShapeDtypeStruct((p, n1, n2, e), BF16)
    return pl.pallas_call(
        _fft_c1_kernel,
        grid=(p, n2 // g, e // ec),
        in_specs=[cspec, cspec,
                  pl.BlockSpec(tc.shape, lambda pi, j, c: (0, 0)),
                  zspec(z_comp), zspec(gate_comp),
                  pl.BlockSpec((1, ec), lambda pi, j, c: (0, c)),
                  pl.BlockSpec((g,) + ta.shape[1:], lambda pi, j, c: (j, 0, 0))],
        out_specs=[pl.BlockSpec((None, two, g, h1, ec), lambda pi, j, c: (pi, 0, j, 0, c)),
                   aspec, aspec],
        out_shape=[jax.ShapeDtypeStruct((p, two, n2, h1, e), BF16), aout, aout],
        compiler_params=_cparams("parallel", "arbitrary", "arbitrary"),
        name="fft_c1",
    )(cre, cim, tc, u6, u6, hb, ta)


def _fft_c2(cre, cim, tc, z1, u6, gate_comp, p6, sg_comp, hb, ec=FFT_EC, g=FFT_G):
    p, n2, n1, e = cre.shape
    _, _, two, _, h1, _ = u6.shape
    ec = min(ec, e)
    cspec = pl.BlockSpec((None, g, n1, ec), lambda pi, j, c: (pi, j, 0, c))
    zspec = lambda comp: pl.BlockSpec((None, None, two, g, h1, ec),
                                      lambda pi, j, c: (comp, pi, 0, j, 0, c))
    z1spec = pl.BlockSpec((None, two, g, h1, ec), lambda pi, j, c: (pi, 0, j, 0, c))
    return pl.pallas_call(
        _fft_c2_kernel,
        grid=(p, n2 // g, e // ec),
        in_specs=[cspec, cspec,
                  pl.BlockSpec(tc.shape, lambda pi, j, c: (0, 0)),
                  z1spec, zspec(gate_comp),
                  pl.BlockSpec((1, ec), lambda pi, j, c: (0, c)),
                  zspec(sg_comp)],
        out_specs=z1spec,
        out_shape=jax.ShapeDtypeStruct((p, two, n2, h1, e), BF16),
        compiler_params=_cparams("parallel", "arbitrary", "arbitrary"),
        name="fft_c2",
    )(cre, cim, tc, z1, u6, hb, p6)


def _attn_kernel(slope_ref, sink_ref, q_ref, kp_ref, kc_ref, kn_ref, vp_ref, vc_ref, vn_ref,
                 glo_ref, ghi_ref, o_ref, *, group):
    n = pl.program_id(1)
    nb = pl.num_programs(1)
    blk = q_ref.shape[0]
    hd = HEAD_DIM
    n_kv = kc_ref.shape[1] // hd
    heads_per_g = glo_ref.shape[1] // hd
    nk = 3 * blk
    sj = lax.broadcasted_iota(jnp.int32, (nk, blk), 0)
    qi = lax.broadcasted_iota(jnp.int32, (nk, blk), 1)
    dist = jnp.abs(qi - sj + blk)
    valid = (dist <= WINDOW) & ((sj >= blk) | (n > 0)) & ((sj < 2 * blk) | (n < nb - 1))
    distm = jnp.where(valid, dist.astype(F32), MASK_DIST)
    pad = BF16_ROWS
    first_q = lax.broadcasted_iota(jnp.int32, (pad, blk), 0) == 0
    first_d = lax.broadcasted_iota(jnp.int32, (pad, hd), 0) == 0
    ones_blk = jnp.ones((nk, hd), BF16)
    sink_v = jnp.concatenate([jnp.zeros((pad, hd), BF16),
                              jnp.where(first_d, 1.0, 0.0).astype(BF16)], axis=1)
    for kvh in range(n_kv):
        ksl = slice(kvh * hd, (kvh + 1) * hd)
        k = jnp.concatenate([kp_ref[:, ksl], kc_ref[:, ksl], kn_ref[:, ksl]], axis=0)
        v = jnp.concatenate([vp_ref[:, ksl], vc_ref[:, ksl], vn_ref[:, ksl]], axis=0)
        v_aug = jnp.concatenate([jnp.concatenate([v, ones_blk], axis=1), sink_v], axis=0)
        heads = range(kvh * group, (kvh + 1) * group)
        q = jnp.concatenate([q_ref[:, h * hd:(h + 1) * hd] for h in heads], axis=0)
        st = lax.dot_general(k, q, (((1,), (1,)), ((), ())), preferred_element_type=F32)
        pts = []
        for gi, h in enumerate(heads):
            logit = st[:, gi * blk:(gi + 1) * blk] - slope_ref[h] * distm
            sk = sink_ref[h]
            m = jnp.maximum(jnp.max(logit, axis=0, keepdims=True), sk)
            p = jnp.exp2(logit - m)
            sink_p = jnp.where(first_q, jnp.exp2(sk - m), 0.0)
            pts.append(jnp.concatenate([p, sink_p], axis=0).astype(BF16))
        oa = lax.dot_general(jnp.concatenate(pts, axis=1), v_aug, (((0,), (0,)), ((), ())),
                             preferred_element_type=F32)
        for gi, h in enumerate(heads):
            rows = slice(gi * blk, (gi + 1) * blk)
            o = oa[rows, :hd] * (1.0 / oa[rows, hd:])
            g_half = glo_ref if h < heads_per_g else ghi_ref
            hg = h % heads_per_g
            gate = _silu(g_half[:, hg * hd:(hg + 1) * hd].astype(F32))
            o_ref[:, h * hd:(h + 1) * hd] = (o * gate).astype(o_ref.dtype)


def _attention(proj, slopes, sink, n_heads):
    bsz, l, _ = proj.shape
    hd, blk = HEAD_DIM, WINDOW
    group = n_heads // N_KV_HEADS
    aw = n_heads * hd
    kvw = N_KV_HEADS * hd
    nb = l // blk
    k0 = aw // kvw
    v0 = k0 + 1
    gw = aw // 2
    g0 = (aw + 2 * kvw) // gw
    smem = pl.BlockSpec(memory_space=pltpu.SMEM)
    prev = lambda n: jnp.maximum(n - 1, 0)
    nxt = lambda n: jnp.minimum(n + 1, nb - 1)
    same = lambda n: n
    kv = lambda c0, f: pl.BlockSpec((None, blk, kvw), lambda b, n: (b, f(n), c0))
    gspec = lambda c0: pl.BlockSpec((None, blk, gw), lambda b, n: (b, n, c0))
    assert (aw + 2 * kvw) % gw == 0 and aw % kvw == 0
    return pl.pallas_call(
        functools.partial(_attn_kernel, group=group),
        grid=(bsz, nb),
        in_specs=[smem, smem,
                  pl.BlockSpec((None, blk, aw), lambda b, n: (b, n, 0)),
                  kv(k0, prev), kv(k0, same), kv(k0, nxt),
                  kv(v0, prev), kv(v0, same), kv(v0, nxt),
                  gspec(g0), gspec(g0 + 1)],
        out_specs=pl.BlockSpec((None, blk, aw), lambda b, n: (b, n, 0)),
        out_shape=jax.ShapeDtypeStruct((bsz, l, aw), BF16),
        compiler_params=_cparams("parallel", "arbitrary"),
        name="swa_attention",
    )(slopes, sink, proj, proj, proj, proj, proj, proj, proj, proj, proj)


def _hyena_filter_spectrum(l, e, hy, tabs):
    _, ta_filt, tb_fwd, _, _ = tabs
    n1 = FFT_N1
    n2 = 2 * l // n1
    k_un, ss = _filters(l, e, n2, hy["w_f1"], hy["b_f1"], hy["fr1"], hy["w_f2"], hy["b_f2"],
                        hy["fr2"], hy["w_f3"], hy["b_f3"], hy["fr3"], hy["w_f4"])
    k6 = k_un.reshape(1, HY_ORDER, 1, n2, n1, e)
    are, aim = _fft_a(k6, 0, ta_filt)
    return _fft_b_filter(are, aim, tb_fwd, ss)


def _hyena_layer(x, hy, kf, tabs, ln_g, ln_b, alpha):
    bsz, l, d = x.shape
    e = hy["w_out"].shape[0]
    ta_data, _, tb_fwd, tb_inv, tc = tabs
    kre, kim = kf
    n1 = FFT_N1
    n2 = 2 * l // n1
    h1 = n1 // 2
    proj = _inproj(x.reshape(bsz * l, d), hy["w_in"], hy["b_in"], 4, permute=(l, n2))
    u = _shortconv(proj, hy["w_sc"], hy["b_sc"])
    pair = lambda a: a.reshape(a.shape[0], bsz // 2, 2, n2, h1, e)
    u6, p6 = pair(u), pair(proj)
    are, aim = _fft_a(u6, 0, ta_data)
    cre, cim = _fft_b(are, aim, kre, kim, 0, tb_fwd, tb_inv)
    z1, are, aim = _fft_c1(cre, cim, tc, u6, 0, 1, hy["h_bias"][0:1], ta_data)
    cre, cim = _fft_b(are, aim, kre, kim, 1, tb_fwd, tb_inv)
    y = _fft_c2(cre, cim, tc, z1, u6, 2, p6, 3, hy["h_bias"][1:2])
    out = _outproj_ln_permuted(y.reshape(bsz, n2, h1, e), hy["w_out"], hy["b_out"],
                               x.reshape(bsz, h1, n2, d), ln_g, ln_b, alpha)
    return out.reshape(bsz, l, d)


def _attention_layer(x, at, ln_g, ln_b, alpha):
    bsz, l, d = x.shape
    n_heads = at["sink"].shape[0]
    x2 = x.reshape(bsz * l, d)
    proj = _inproj(x2, at["w_in"], at["b_in"], 1, col_scale=at["col_scale"])
    proj = proj.reshape(bsz, l, -1)
    o = _attention(proj, at["slopes"], at["sink"] * LOG2E, n_heads)
    out = _outproj_ln(o.reshape(bsz * l, -1), at["w_out"], at["b_out"], x2, ln_g, ln_b, alpha)
    return out.reshape(bsz, l, d)


def kernel(x_prompt, x_sample, ln_g, ln_b, hy_w_in, hy_b_in, hy_w_sc, hy_b_sc, hy_w_f1, hy_b_f1,
           hy_fr1, hy_w_f2, hy_b_f2, hy_fr2, hy_w_f3, hy_b_f3, hy_fr3, hy_w_f4, hy_h_bias,
           hy_w_out, hy_b_out, at_w_in, at_sink, at_w_out):
    depth = ln_g.shape[0]
    alpha = (2 * depth) ** 0.25
    l = x_prompt.shape[1]
    d = x_prompt.shape[2]
    assert x_sample.shape[1] == l and (2 * l) % FFT_N1 == 0
    assert x_prompt.shape[0] % 2 == 0 and x_sample.shape[0] % 2 == 0
    tabs = tuple(jnp.asarray(t).astype(BF16) for t in _dft_tables(2 * l))
    row = lambda a: a.reshape(1, -1)

    layers = []
    for i in range(depth):
        j = i // N_MIXERS
        if i % N_MIXERS == 0:
            e = hy_w_out.shape[1]
            hy = dict(w_in=hy_w_in[j].astype(BF16), b_in=row(hy_b_in[j]), w_sc=hy_w_sc[j],
                      b_sc=row(hy_b_sc[j]), w_f1=hy_w_f1[j], b_f1=hy_b_f1[j], fr1=hy_fr1[j],
                      w_f2=hy_w_f2[j], b_f2=hy_b_f2[j], fr2=hy_fr2[j], w_f3=hy_w_f3[j],
                      b_f3=hy_b_f3[j], fr3=hy_fr3[j], w_f4=hy_w_f4[j], h_bias=hy_h_bias[j],
                      w_out=hy_w_out[j].astype(BF16), b_out=row(hy_b_out[j]))
            kf = _hyena_filter_spectrum(l, e, hy, tabs)
            layers.append(("hyena", hy, kf))
        else:
            n_heads = at_sink.shape[1]
            aw = n_heads * HEAD_DIM
            width = at_w_in.shape[2]
            col_scale = jnp.where(jnp.arange(width) < aw, HEAD_DIM ** -0.5 * LOG2E, 1.0)
            slopes = jnp.exp2(-8.0 * jnp.arange(1, n_heads + 1, dtype=F32) / n_heads) * LOG2E
            at = dict(w_in=at_w_in[j].astype(BF16), b_in=jnp.zeros((1, width), F32),
                      col_scale=row(col_scale.astype(F32)), slopes=slopes,
                      sink=at_sink[j], w_out=at_w_out[j].astype(BF16),
                      b_out=jnp.zeros((1, d), F32))
            layers.append(("attn", at, None))

    def trunk(x):
        for i, (kind, prm, kf) in enumerate(layers):
            g, b = row(ln_g[i]), row(ln_b[i])
            if kind == "hyena":
                x = _hyena_layer(x, prm, kf, tabs, g, b, alpha)
            else:
                x = _attention_layer(x, prm, g, b, alpha)
        return x

    return (trunk(x_prompt), trunk(x_sample))
```

```python
import functools
import math

import numpy as np
import jax
import jax.numpy as jnp
from jax import lax
from jax.experimental import pallas as pl
from jax.experimental.pallas import tpu as pltpu

F32 = jnp.float32
BF16 = jnp.bfloat16

N_MIXERS = 2
HY_ORDER = 2
HY_DIRS = 2
SHORT_CONV = 3
POS_EMB_DIM = 33
POS_BANDS = (POS_EMB_DIM - 1) // 2
DECAY_TARGET = 1e-2
FAST_DECAY_PCT = 0.3
SLOW_DECAY_PCT = 1.5
MIN_DECAY = math.log(DECAY_TARGET) / SLOW_DECAY_PCT
MAX_DECAY = math.log(DECAY_TARGET) / FAST_DECAY_PCT
N_KV_HEADS = 4
HEAD_DIM = 128
WINDOW = 128
LN_EPS = 1e-5
LOG2E = 1.4426950408889634
MASK_DIST = 1e30

FFT_N1 = 128
LANE = 128
BF16_ROWS = 16
F32_ROWS = 8
VMEM_LIMIT = 56 * 1024 * 1024


def _cparams(*sem):
    return pltpu.CompilerParams(dimension_semantics=sem, vmem_limit_bytes=VMEM_LIMIT)


def _swap_major_sublane(x):
    return jnp.swapaxes(x, 0, 1)


def _silu(x):
    return x * (0.5 * jnp.tanh(0.5 * x) + 0.5)


def _block(m):
    return np.block([[m.real, -m.imag], [m.imag, m.real]])


@functools.lru_cache(maxsize=None)
def _dft_tables(n):
    n1, n2 = FFT_N1, n // FFT_N1
    h1 = n1 // 2
    k1 = np.arange(n1)[:, None]
    ta_data, ta_filt = [], []
    for j in range(n2):
        m = np.exp(-2j * np.pi * (j * k1 / n + np.arange(n1)[None, :] * k1 / n1))
        ta_data.append(_block(m[:, :h1]))
        ta_filt.append(np.concatenate([m.real, m.imag], 0))
    f2 = np.exp(-2j * np.pi * np.arange(n2)[:, None] * np.arange(n2)[None, :] / n2)
    tb_fwd = _block(f2)
    t1 = np.arange(n2)[:, None]
    tb_inv = [_block(np.exp(2j * np.pi * (kk * t1 / n + np.arange(n2)[None, :] * t1 / n2)))
              for kk in range(n1)]
    hc = np.exp(2j * np.pi * np.arange(h1)[:, None] * np.arange(n1)[None, :] / n1) / n
    tc = _block(hc)
    f = lambda a: np.asarray(a, np.float32)
    pair = lambda t: np.concatenate([t[0::2], t[1::2]], axis=2)
    ta_data, tb_inv = np.stack(ta_data), np.stack(tb_inv)
    return dict(ta_data=f(ta_data), ta_filt=f(np.stack(ta_filt)), tb_fwd=f(tb_fwd), tb_inv=f(tb_inv),
                tc=f(tc), ta_pair=f(pair(ta_data)), tb_inv_pair=f(pair(tb_inv)),
                tb_fwd_pair=f(np.concatenate([tb_fwd, tb_fwd], axis=1)))


def _inproj_kernel(x_ref, w_ref, b_ref, cs_ref, o_ref, xb_ref, *, scaled):
    @pl.when(pl.program_id(1) == 0)
    def _():
        xb_ref[...] = x_ref[...].astype(BF16)

    acc = jnp.dot(xb_ref[...], w_ref[...], preferred_element_type=F32)
    if scaled:
        acc = acc * cs_ref[...]
    val = (acc + b_ref[...]).astype(o_ref.dtype)
    if len(o_ref.shape) == 3:
        n2, n1b, tn = o_ref.shape
        val = _swap_major_sublane(val.reshape(n1b, n2, tn))
    o_ref[...] = val


def _inproj(x, w, b, n_comp, col_scale=None, permute=None, tm=1024, tn=1024):
    m, k = x.shape
    n = w.shape[1]
    wc = n // n_comp
    tn = min(tn, wc)
    ncb = wc // tn
    scaled = col_scale is not None
    if not scaled:
        col_scale = b
    if permute is None:
        out_spec = pl.BlockSpec((None, tm, tn), lambda i, j: (j // ncb, i, j % ncb))
        out_shape = jax.ShapeDtypeStruct((n_comp, m, wc), BF16)
    else:
        l, n2 = permute
        tm = BF16_ROWS * n2
        tpb = l // tm
        out_spec = pl.BlockSpec((None, None, n2, BF16_ROWS, tn),
                                lambda i, j: (j // ncb, i // tpb, 0, i % tpb, j % ncb))
        out_shape = jax.ShapeDtypeStruct((n_comp, m // l, n2, l // n2, wc), BF16)
    vec = pl.BlockSpec((1, tn), lambda i, j: (0, j))
    return pl.pallas_call(
        functools.partial(_inproj_kernel, scaled=scaled),
        grid=(m // tm, n // tn),
        in_specs=[
            pl.BlockSpec((tm, k), lambda i, j: (i, 0)),
            pl.BlockSpec((k, tn), lambda i, j: (0, j)),
            vec, vec,
        ],
        out_specs=out_spec,
        out_shape=out_shape,
        scratch_shapes=[pltpu.VMEM((tm, k), BF16)],
        compiler_params=_cparams("parallel", "arbitrary"),
        name="inproj",
    )(x, w, b, col_scale)


def _outproj_ln_kernel(y_ref, w_ref, b_ref, x_ref, g_ref, beta_ref, o_ref, *, alpha, permuted):
    y = y_ref[...]
    if permuted:
        nb, na, kk = y.shape
        y = y.reshape(nb * na, kk)
    h = jnp.dot(y, w_ref[...], preferred_element_type=F32) + b_ref[...]
    if permuted:
        h = _swap_major_sublane(h.reshape(nb, na, h.shape[-1]))
    r = alpha * x_ref[...] + h
    mu = jnp.mean(r, axis=-1, keepdims=True)
    d = r - mu
    var = jnp.mean(d * d, axis=-1, keepdims=True)
    o_ref[...] = d * lax.rsqrt(var + LN_EPS) * g_ref[...] + beta_ref[...]


def _outproj_ln(y, w, b, x, g, beta, alpha, tm=512):
    m, k = y.shape
    d = w.shape[1]
    row = lambda i: (i, 0)
    fixed = lambda i: (0, 0)
    return pl.pallas_call(
        functools.partial(_outproj_ln_kernel, alpha=alpha, permuted=False),
        grid=(m // tm,),
        in_specs=[
            pl.BlockSpec((tm, k), row),
            pl.BlockSpec((k, d), fixed),
            pl.BlockSpec((1, d), fixed),
            pl.BlockSpec((tm, d), row),
            pl.BlockSpec((1, d), fixed),
            pl.BlockSpec((1, d), fixed),
        ],
        out_specs=pl.BlockSpec((tm, d), row),
        out_shape=jax.ShapeDtypeStruct((m, d), F32),
        compiler_params=_cparams("parallel"),
        name="outproj_ln",
    )(y, w, b, x, g, beta)


def _outproj_ln_permuted(y, w, b, x, g, beta, alpha):
    bsz, n2, h1, k = y.shape
    d = w.shape[1]
    nb = F32_ROWS
    fixed = lambda bi, j: (0, 0)
    xspec = pl.BlockSpec((None, h1, nb, d), lambda bi, j: (bi, 0, j, 0))
    return pl.pallas_call(
        functools.partial(_outproj_ln_kernel, alpha=alpha, permuted=True),
        grid=(bsz, n2 // nb),
        in_specs=[
            pl.BlockSpec((None, nb, h1, k), lambda bi, j: (bi, j, 0, 0)),
            pl.BlockSpec((k, d), fixed),
            pl.BlockSpec((1, d), fixed),
            xspec,
            pl.BlockSpec((1, d), fixed),
            pl.BlockSpec((1, d), fixed),
        ],
        out_specs=xspec,
        out_shape=jax.ShapeDtypeStruct((bsz, h1, n2, d), F32),
        compiler_params=_cparams("parallel", "parallel"),
        name="outproj_ln_perm",
    )(y, w, b, x, g, beta)


def _shortconv_kernel(cur_ref, prev_ref, next_ref, w_ref, b_ref, o_ref):
    i = pl.program_id(2)
    last = pl.num_programs(2) - 1
    n2, n1b, tc = cur_ref.shape
    cur = cur_ref[...].astype(F32)
    row = lax.broadcasted_iota(jnp.int32, (n1b, tc), 0)
    prev_row = jnp.where(i > 0, prev_ref[0, n1b - 1:n1b, :].astype(F32), 0.0)
    lo = jnp.where(row == 0, prev_row, pltpu.roll(cur[n2 - 1], 1, axis=0))
    next_row = jnp.where(i < last, next_ref[0, 0:1, :].astype(F32), 0.0)
    hi = jnp.where(row == n1b - 1, next_row, pltpu.roll(cur[0], n1b - 1, axis=0))
    down = jnp.concatenate([lo[None], cur[:n2 - 1]], axis=0)
    up = jnp.concatenate([cur[1:], hi[None]], axis=0)
    out = b_ref[...] + down * w_ref[0:1, :] + cur * w_ref[1:2, :] + up * w_ref[2:3, :]
    o_ref[...] = out.astype(o_ref.dtype)


def _shortconv(proj, w_sc, b_sc, tc=1024):
    _, bsz, n2, h1, e = proj.shape
    ncomp = w_sc.shape[1] // e
    n1b = BF16_ROWS
    tc = min(tc, e)
    ncb = e // tc
    nblk = h1 // n1b
    wcol = lambda c, b, i, j: (0, c * ncb + j)
    blk = lambda c, b, i, j: (c, b, 0, i, j)
    return pl.pallas_call(
        _shortconv_kernel,
        grid=(ncomp, bsz, nblk, ncb),
        in_specs=[
            pl.BlockSpec((None, None, n2, n1b, tc), blk),
            pl.BlockSpec((None, None, 1, n1b, tc),
                         lambda c, b, i, j: (c, b, n2 - 1, jnp.maximum(i - 1, 0), j)),
            pl.BlockSpec((None, None, 1, n1b, tc),
                         lambda c, b, i, j: (c, b, 0, jnp.minimum(i + 1, nblk - 1), j)),
            pl.BlockSpec((SHORT_CONV, tc), wcol),
            pl.BlockSpec((1, tc), wcol),
        ],
        out_specs=pl.BlockSpec((None, None, n2, n1b, tc), blk),
        out_shape=jax.ShapeDtypeStruct((ncomp, bsz, n2, h1, e), BF16),
        compiler_params=_cparams("parallel", "parallel", "arbitrary", "arbitrary"),
        name="shortconv",
    )(proj, proj, proj, w_sc, b_sc)


def _filter_kernel(feat_ref, w1_ref, b1_ref, fr1_ref, w2_ref, b2_ref, fr2_ref, w3_ref, b3_ref,
                   fr3_ref, w4_ref, delta_ref, k_ref, ss_ref, h_ref):
    half = pl.program_id(0)
    hp = lax.Precision.HIGHEST

    @pl.when((pl.program_id(1) == 0) & (pl.program_id(2) == 0))
    def _():
        h = jnp.sin(fr1_ref[...] * (jnp.dot(feat_ref[...], w1_ref[...], precision=hp,
                                             preferred_element_type=F32) + b1_ref[...]))
        h = jnp.sin(fr2_ref[...] * (jnp.dot(h, w2_ref[...], precision=hp,
                                             preferred_element_type=F32) + b2_ref[...]))
        h = jnp.sin(fr3_ref[...] * (jnp.dot(h, w3_ref[...], precision=hp,
                                             preferred_element_type=F32) + b3_ref[...]))
        h_hi = h.astype(BF16)
        h_lo = (h - h_hi.astype(F32)).astype(BF16)
        h_ref[...] = jnp.concatenate([h_hi, h_hi, h_lo], axis=1)

    w4 = w4_ref[...]
    w_hi = w4.astype(BF16)
    w_lo = (w4 - w_hi.astype(F32)).astype(BF16)
    k = jnp.dot(h_ref[...], jnp.concatenate([w_hi, w_lo, w_hi], axis=0), preferred_element_type=F32)
    t_norm = feat_ref[:, 0:1]
    k = k * jnp.exp(-t_norm * delta_ref[...])
    row = lax.broadcasted_iota(jnp.int32, k.shape, 0)
    k = jnp.where((row == 0) & (half == 1), 0.0, k)
    ss_ref[...] = jnp.sum(k * k, axis=0, keepdims=True)
    n2, rows, ec = k_ref.shape
    k_ref[...] = _swap_major_sublane(k.astype(k_ref.dtype).reshape(rows, n2, ec))


def _filters(l, e, n2, w_f1, b_f1, fr1, w_f2, b_f2, fr2, w_f3, b_f3, fr3, w_f4, ec=512):
    fh = w_f1.shape[1]
    ec = min(ec, e)
    t_norm = jnp.linspace(0.0, 1.0, l, dtype=F32)
    w = 2.0 * math.pi * jnp.arange(l, dtype=F32) / l
    f = jnp.linspace(1e-4, POS_BANDS - 1, POS_BANDS, dtype=F32)
    ang = w[:, None] * f[None, :]
    feat = jnp.concatenate([t_norm[:, None], jnp.cos(ang), -jnp.sin(ang)], axis=-1)
    feat_rev = jnp.concatenate([feat[:1], feat[:0:-1]], axis=0)
    feat2 = jnp.pad(jnp.concatenate([feat, feat_rev], axis=0), ((0, 0), (0, LANE - POS_EMB_DIM)))
    w1p = jnp.pad(w_f1, ((0, LANE - POS_EMB_DIM), (0, 0)))
    deltas = jnp.abs(jnp.linspace(MIN_DECAY, MAX_DECAY, e, dtype=F32))[None, :]
    nec = e // ec
    rows = l // n2
    vec = lambda a: a.reshape(1, -1)
    fixed = lambda h, o, j: (0, 0)
    return pl.pallas_call(
        _filter_kernel,
        grid=(HY_DIRS, HY_ORDER, nec),
        in_specs=[
            pl.BlockSpec((l, LANE), lambda h, o, j: (h, 0)),
            pl.BlockSpec((LANE, fh), fixed), pl.BlockSpec((1, fh), fixed), pl.BlockSpec((1, fh), fixed),
            pl.BlockSpec((fh, fh), fixed), pl.BlockSpec((1, fh), fixed), pl.BlockSpec((1, fh), fixed),
            pl.BlockSpec((fh, fh), fixed), pl.BlockSpec((1, fh), fixed), pl.BlockSpec((1, fh), fixed),
            pl.BlockSpec((fh, ec), lambda h, o, j: (0, (o * HY_DIRS + h) * nec + j)),
            pl.BlockSpec((1, ec), lambda h, o, j: (0, j)),
        ],
        out_specs=[
            pl.BlockSpec((None, n2, rows, ec), lambda h, o, j: (o, 0, h, j)),
            pl.BlockSpec((None, None, 1, ec), lambda h, o, j: (o, h, 0, j)),
        ],
        out_shape=[
            jax.ShapeDtypeStruct((HY_ORDER, n2, HY_DIRS * rows, e), BF16),
            jax.ShapeDtypeStruct((HY_ORDER, HY_DIRS, 1, e), F32),
        ],
        scratch_shapes=[pltpu.VMEM((l, 3 * fh), BF16)],
        compiler_params=_cparams("arbitrary", "arbitrary", "arbitrary"),
        name="hyena_filter",
    )(feat2, w1p, vec(b_f1), vec(fr1), w_f2, vec(b_f2), vec(fr2), w_f3, vec(b_f3), vec(fr3),
      w_f4, deltas)


FFT_G = BF16_ROWS
FFT_EC = 512


def _stage_a(z_ref, ta_ref, are_ref, aim_ref, src=None):
    g = ta_ref.shape[0]
    n1 = are_ref.shape[0]
    rs = []
    for j in range(g):
        z = src[j] if src is not None else z_ref[:, j].reshape(ta_ref.shape[2], z_ref.shape[-1])
        rs.append(jnp.dot(ta_ref[j], z, preferred_element_type=F32).astype(BF16))
    rt = _swap_major_sublane(jnp.stack(rs))
    are_ref[...] = rt[:n1]
    aim_ref[...] = rt[n1:]


def _fft_a_kernel(z_ref, ta_ref, are_ref, aim_ref):
    _stage_a(z_ref, ta_ref, are_ref, aim_ref)


def _fft_a(z6, comp, ta, ec=FFT_EC, g=FFT_G):
    _, p, s, n2, r, e = z6.shape
    _, m2, kk = ta.shape
    n1 = m2 // 2
    ec = min(ec, e)
    out = jax.ShapeDtypeStruct((p, n1, n2, e), BF16)
    ospec = pl.BlockSpec((None, n1, g, ec), lambda pi, j, c: (pi, 0, j, c))
    return pl.pallas_call(
        _fft_a_kernel,
        grid=(p, n2 // g, e // ec),
        in_specs=[
            pl.BlockSpec((None, None, s, g, r, ec), lambda pi, j, c: (comp, pi, 0, j, 0, c)),
            pl.BlockSpec((g, m2, kk), lambda pi, j, c: (j, 0, 0)),
        ],
        out_specs=[ospec, ospec],
        out_shape=[out, out],
        compiler_params=_cparams("parallel", "arbitrary", "arbitrary"),
        name="fft_a",
    )(z6, ta)


def _fft_bf_kernel(are_ref, aim_ref, tf_ref, ss_ref, kre_ref, kim_ref):
    g, n2, _ = are_ref.shape
    rs = lax.rsqrt(ss_ref[0] + ss_ref[1] + 1e-12)
    for j in range(g):
        a = jnp.concatenate([are_ref[j], aim_ref[j]], axis=0)
        x = jnp.dot(tf_ref[...], a, preferred_element_type=F32) * rs
        kre_ref[j] = x[:n2].astype(kre_ref.dtype)
        kim_ref[j] = x[n2:].astype(kim_ref.dtype)


def _fft_b_filter(are, aim, tf, ss, ec=FFT_EC, g=FFT_G):
    p, n1, n2, e = are.shape
    ec = min(ec, e)
    spec = pl.BlockSpec((None, g, n2, ec), lambda j, c, pi: (pi, j, 0, c))
    out = jax.ShapeDtypeStruct((p, n1, n2, e), BF16)
    return pl.pallas_call(
        _fft_bf_kernel,
        grid=(n1 // g, e // ec, p),
        in_specs=[spec, spec,
                  pl.BlockSpec((2 * n2, 2 * n2), lambda j, c, pi: (0, 0)),
                  pl.BlockSpec((None, HY_DIRS, 1, ec), lambda j, c, pi: (pi, 0, 0, c))],
        out_specs=[spec, spec],
        out_shape=[out, out],
        compiler_params=_cparams("parallel", "arbitrary", "arbitrary"),
        name="fft_b_filter",
    )(are, aim, tf, ss)


def _fft_b_kernel(are_ref, aim_ref, kre_ref, kim_ref, tf_ref, tg_ref, cre_ref, cim_ref):
    g, n2, _ = are_ref.shape
    xs = [jnp.dot(tf_ref[...], jnp.concatenate([are_ref[j], aim_ref[j]], axis=0),
                  preferred_element_type=F32) for j in range(g)]
    ys = []
    for j in range(g):
        xr, xi = xs[j][:n2], xs[j][n2:]
        kr = kre_ref[j].astype(F32)
        ki = kim_ref[j].astype(F32)
        ys.append(jnp.concatenate([xr * kr - xi * ki, xr * ki + xi * kr], axis=0).astype(BF16))
    cs = [jnp.dot(tg_ref[j], ys[j], preferred_element_type=F32).astype(BF16) for j in range(g)]
    ct = _swap_major_sublane(jnp.stack(cs))
    cre_ref[...] = ct[:n2]
    cim_ref[...] = ct[n2:]


def _fft_b(are, aim, kre, kim, order, tf, tg, ec=FFT_EC, g=FFT_G):
    p, n1, n2, e = are.shape
    ec = min(ec, e)
    dspec = pl.BlockSpec((None, g, n2, ec), lambda j, c, pi: (pi, j, 0, c))
    kspec = pl.BlockSpec((None, g, n2, ec), lambda j, c, pi: (order, j, 0, c))
    ospec = pl.BlockSpec((None, n2, g, ec), lambda j, c, pi: (pi, 0, j, c))
    out = jax.ShapeDtypeStruct((p, n2, n1, e), BF16)
    return pl.pallas_call(
        _fft_b_kernel,
        grid=(n1 // g, e // ec, p),
        in_specs=[dspec, dspec, kspec, kspec,
                  pl.BlockSpec((2 * n2, 2 * n2), lambda j, c, pi: (0, 0)),
                  pl.BlockSpec((g, 2 * n2, 2 * n2), lambda j, c, pi: (j, 0, 0))],
        out_specs=[ospec, ospec],
        out_shape=[out, out],
        compiler_params=_cparams("parallel", "arbitrary", "arbitrary"),
        name="fft_b",
    )(are, aim, kre, kim, tf, tg)


def _stage_c(cre_ref, cim_ref, tc_ref, j):
    c = jnp.concatenate([cre_ref[j], cim_ref[j]], axis=0)
    return jnp.dot(tc_ref[...], c, preferred_element_type=F32)


def _rows(ref, j):
    two, _, h1, ec = ref.shape
    return ref[:, j].reshape(two * h1, ec)


def _fft_c1_kernel(cre_ref, cim_ref, tc_ref, z_ref, gate_ref, hb_ref, ta_ref,
                   z1_ref, are_ref, aim_ref):
    g = cre_ref.shape[0]
    two, _, h1, ec = z_ref.shape
    ys = [_stage_c(cre_ref, cim_ref, tc_ref, j) for j in range(g)]
    z1s = []
    for j in range(g):
        z1 = _rows(gate_ref, j).astype(F32) * (ys[j] + hb_ref[...] * _rows(z_ref, j).astype(F32))
        z1 = z1.astype(BF16)
        z1_ref[:, j] = z1.reshape(two, h1, ec)
        z1s.append(z1)
    _stage_a(None, ta_ref, are_ref, aim_ref, src=z1s)


def _fft_c2_kernel(cre_ref, cim_ref, tc_ref, z_ref, gate_ref, hb_ref, sg_ref, y_ref):
    g = cre_ref.shape[0]
    two, _, h1, ec = z_ref.shape
    ys = [_stage_c(cre_ref, cim_ref, tc_ref, j) for j in range(g)]
    for j in range(g):
        z2 = _rows(gate_ref, j).astype(F32) * (ys[j] + hb_ref[...] * _rows(z_ref, j).astype(F32))
        out = z2 * _silu(_rows(sg_ref, j).astype(F32))
        y_ref[:, j] = out.astype(BF16).reshape(two, h1, ec)


def _fft_c1(cre, cim, tc, u6, z_comp, gate_comp, hb, ta, ec=FFT_EC, g=FFT_G):
    p, n2, n1, e = cre.shape
    _, _, two, _, h1, _ = u6.shape
    ec = min(ec, e)
    cspec = pl.BlockSpec((None, g, n1, ec), lambda pi, j, c: (pi, j, 0, c))
    zspec = lambda comp: pl.BlockSpec((None, None, two, g, h1, ec),
                                      lambda pi, j, c: (comp, pi, 0, j, 0, c))
    aspec = pl.BlockSpec((None, n1, g, ec), lambda pi, j, c: (pi, 0, j, c))
    aout = jax.ShapeDtypeStruct((p, n1, n2, e), BF16)
    return pl.pallas_call(
        _fft_c1_kernel,
        grid=(p, n2 // g, e // ec),
        in_specs=[cspec, cspec,
                  pl.BlockSpec(tc.shape, lambda pi, j, c: (0, 0)),
                  zspec(z_comp), zspec(gate_comp),
                  pl.BlockSpec((1, ec), lambda pi, j, c: (0, c)),
                  pl.BlockSpec((g,) + ta.shape[1:], lambda pi, j, c: (j, 0, 0))],
        out_specs=[pl.BlockSpec((None, two, g, h1, ec), lambda pi, j, c: (pi, 0, j, 0, c)),
                   aspec, aspec],
        out_shape=[jax.ShapeDtypeStruct((p, two, n2, h1, e), BF16), aout, aout],
        compiler_params=_cparams("parallel", "arbitrary", "arbitrary"),
        name="fft_c1",
    )(cre, cim, tc, u6, u6, hb, ta)


def _fft_c2(cre, cim, tc, z1, u6, gate_comp, p6, sg_comp, hb, ec=FFT_EC, g=FFT_G):
    p, n2, n1, e = cre.shape
    _, _, two, _, h1, _ = u6.shape
    ec = min(ec, e)
    cspec = pl.BlockSpec((None, g, n1, ec), lambda pi, j, c: (pi, j, 0, c))
    zspec = lambda comp: pl.BlockSpec((None, None, two, g, h1, ec),
                                      lambda pi, j, c: (comp, pi, 0, j, 0, c))
    z1spec = pl.BlockSpec((None, two, g, h1, ec), lambda pi, j, c: (pi, 0, j, 0, c))
    return pl.pallas_call(
        _fft_c2_kernel,
        grid=(p, n2 // g, e // ec),
        in_specs=[cspec, cspec,
                  pl.BlockSpec(tc.shape, lambda pi, j, c: (0, 0)),
                  z1spec, zspec(gate_comp),
                  pl.BlockSpec((1, ec), lambda pi, j, c: (0, c)),
                  zspec(sg_comp)],
        out_specs=z1spec,
        out_shape=jax.ShapeDtypeStruct((p, two, n2, h1, e), BF16),
        compiler_params=_cparams("parallel", "arbitrary", "arbitrary"),
        name="fft_c2",
    )(cre, cim, tc, z1, u6, hb, p6)


HC_EC = LANE
HC_G = BF16_ROWS
STAGE_B_SPLIT = 1


def _blockdiag(a, b):
    z = jnp.zeros_like(a)
    return jnp.concatenate([jnp.concatenate([a, z], axis=1),
                            jnp.concatenate([z, b], axis=1)], axis=0)


def _hyena_core_kernel(p_ref, wsc_ref, bsc_ref, hb_ref, kre_ref, kim_ref, tap_ref, tfp_ref,
                       tgp_ref, tc_ref, y_ref, a_scr, c_scr, z_scr):
    _, two, n2, h1, ec = p_ref.shape
    n1, g = a_scr.shape[2], a_scr.shape[3]
    rows = two * h1
    ngrp_a, ngrp_b = n2 // g, n1 // g
    half = g // 2
    rowi = lax.broadcasted_iota(jnp.int32, (rows, ec), 0) & (h1 - 1)

    def slab(comp, idx):
        return p_ref[comp, :, idx].reshape(rows, ec).astype(F32)

    def conv_group(comp, gi):
        base = gi * g
        first, last = gi == 0, gi == ngrp_a - 1
        lo = slab(comp, jnp.where(first, n2 - 1, base - 1))
        lo = jnp.where(first, jnp.where(rowi == 0, 0.0, pltpu.roll(lo, 1, axis=0)), lo)
        hi = slab(comp, jnp.where(last, 0, base + g))
        hi = jnp.where(last, jnp.where(rowi == h1 - 1, 0.0, pltpu.roll(hi, rows - 1, axis=0)), hi)
        vals = [lo] + [slab(comp, base + j) for j in range(g)] + [hi]
        w = [wsc_ref[t, comp:comp + 1, :] for t in range(SHORT_CONV)]
        b = bsc_ref[comp:comp + 1, :]
        return [b + w[0] * vals[j] + w[1] * vals[j + 1] + w[2] * vals[j + 2] for j in range(g)]

    def stage_a_group(gi, zs):
        outs = []
        for jj in range(half):
            r = jnp.dot(tap_ref[gi * half + jj], _blockdiag(zs[2 * jj], zs[2 * jj + 1]),
                        preferred_element_type=F32).astype(BF16)
            outs += [r[:, :ec], r[:, ec:]]
        rt = _swap_major_sublane(jnp.stack(outs))
        a_scr[0, gi] = rt[:n1]
        a_scr[1, gi] = rt[n1:]

    def stage_b(order):
        def col(k1):
            return jnp.concatenate([a_scr[0, :, k1].reshape(n2, ec),
                                    a_scr[1, :, k1].reshape(n2, ec)], axis=0)

        def body(kb, carry):
            cs = []
            sub = half // STAGE_B_SPLIT
            for s0 in range(0, half, sub):
                pairs = range(s0, s0 + sub)
                xs = [jnp.dot(tfp_ref[...], _blockdiag(col(kb * g + 2 * jj), col(kb * g + 2 * jj + 1)),
                              preferred_element_type=F32) for jj in pairs]
                rhs = []
                for x, jj in zip(xs, pairs):
                    ka = kb * g + 2 * jj
                    kr = jnp.concatenate([kre_ref[order, ka], kre_ref[order, ka + 1]], axis=1).astype(F32)
                    ki = jnp.concatenate([kim_ref[order, ka], kim_ref[order, ka + 1]], axis=1).astype(F32)
                    xr, xi = x[:n2], x[n2:]
                    yr = (xr * kr - xi * ki).astype(BF16)
                    yi = (xr * ki + xi * kr).astype(BF16)
                    rhs.append(_blockdiag(jnp.concatenate([yr[:, :ec], yi[:, :ec]], axis=0),
                                          jnp.concatenate([yr[:, ec:], yi[:, ec:]], axis=0)))
                for r, jj in zip(rhs, pairs):
                    c = jnp.dot(tgp_ref[kb * half + jj], r,
                                preferred_element_type=F32).astype(BF16)
                    cs += [c[:, :ec], c[:, ec:]]
            ct = _swap_major_sublane(jnp.stack(cs))
            c_scr[0, kb] = ct[:n2]
            c_scr[1, kb] = ct[n2:]
            return carry

        lax.fori_loop(0, ngrp_b, body, 0)

    def stage_c_slab(t1):
        c = jnp.concatenate([c_scr[0, :, t1].reshape(n1, ec), c_scr[1, :, t1].reshape(n1, ec)], axis=0)
        return jnp.dot(tc_ref[...], c, preferred_element_type=F32)

    def body_a1(gi, carry):
        zs = [v.astype(BF16) for v in conv_group(0, gi)]
        for j in range(g):
            z_scr[gi * g + j] = zs[j]
        stage_a_group(gi, zs)
        return carry

    def body_c1(gi, carry):
        gate = conv_group(1, gi)
        zs = []
        for j in range(g):
            t1 = gi * g + j
            z1 = gate[j] * (stage_c_slab(t1) + hb_ref[0:1, :] * z_scr[t1].astype(F32))
            zs.append(z1.astype(BF16))
            z_scr[t1] = zs[j]
        stage_a_group(gi, zs)
        return carry

    def body_c2(gi, carry):
        gate = conv_group(2, gi)
        for j in range(g):
            t1 = gi * g + j
            z2 = gate[j] * (stage_c_slab(t1) + hb_ref[1:2, :] * z_scr[t1].astype(F32))
            out = z2 * _silu(slab(3, t1))
            y_ref[:, t1] = out.astype(BF16).reshape(two, h1, ec)
        return carry

    lax.fori_loop(0, ngrp_a, body_a1, 0)
    stage_b(0)
    lax.fori_loop(0, ngrp_a, body_c1, 0)
    stage_b(1)
    lax.fori_loop(0, ngrp_a, body_c2, 0)


def _hyena_core(p6, w_sc, b_sc, h_bias, kre, kim, tap, tfp, tgp, tc, ec=HC_EC, g=HC_G):
    _, p, two, n2, h1, e = p6.shape
    n1 = kre.shape[1]
    ec = min(ec, e)
    once = pl.Buffered(1)
    const = lambda shape: pl.BlockSpec(shape, lambda c, pi: (0,) * len(shape), pipeline_mode=once)
    kspec = pl.BlockSpec((HY_ORDER, n1, n2, ec), lambda c, pi: (0, 0, 0, c), pipeline_mode=once)
    return pl.pallas_call(
        _hyena_core_kernel,
        grid=(e // ec, p),
        in_specs=[
            pl.BlockSpec((4, None, two, n2, h1, ec), lambda c, pi: (0, pi, 0, 0, 0, c)),
            pl.BlockSpec((SHORT_CONV, 3, ec), lambda c, pi: (0, 0, c)),
            pl.BlockSpec((3, ec), lambda c, pi: (0, c)),
            pl.BlockSpec((HY_ORDER, ec), lambda c, pi: (0, c)),
            kspec, kspec,
            const(tap.shape), const(tfp.shape), const(tgp.shape), const(tc.shape),
        ],
        out_specs=pl.BlockSpec((None, two, n2, h1, ec), lambda c, pi: (pi, 0, 0, 0, c)),
        out_shape=jax.ShapeDtypeStruct((p, two, n2, h1, e), BF16),
        scratch_shapes=[pltpu.VMEM((2, n2 // g, n1, g, ec), BF16),
                        pltpu.VMEM((2, n1 // g, n2, g, ec), BF16),
                        pltpu.VMEM((n2, two * h1, ec), BF16)],
        compiler_params=_cparams("parallel", "arbitrary"),
        name="hyena_core",
    )(p6, w_sc.reshape(SHORT_CONV, 3, e), b_sc.reshape(3, e), h_bias, kre, kim, tap, tfp, tgp, tc)


def _attn_kernel(slope_ref, sink_ref, q_ref, kp_ref, kc_ref, kn_ref, vp_ref, vc_ref, vn_ref,
                 glo_ref, ghi_ref, o_ref, *, group):
    n = pl.program_id(1)
    nb = pl.num_programs(1)
    blk = q_ref.shape[0]
    hd = HEAD_DIM
    n_kv = kc_ref.shape[1] // hd
    heads_per_g = glo_ref.shape[1] // hd
    nk = 3 * blk
    sj = lax.broadcasted_iota(jnp.int32, (nk, blk), 0)
    qi = lax.broadcasted_iota(jnp.int32, (nk, blk), 1)
    dist = jnp.abs(qi - sj + blk)
    valid = (dist <= WINDOW) & ((sj >= blk) | (n > 0)) & ((sj < 2 * blk) | (n < nb - 1))
    distm = jnp.where(valid, dist.astype(F32), MASK_DIST)
    pad = BF16_ROWS
    first_q = lax.broadcasted_iota(jnp.int32, (pad, blk), 0) == 0
    first_d = lax.broadcasted_iota(jnp.int32, (pad, hd), 0) == 0
    ones_blk = jnp.ones((nk, hd), BF16)
    sink_v = jnp.concatenate([jnp.zeros((pad, hd), BF16),
                              jnp.where(first_d, 1.0, 0.0).astype(BF16)], axis=1)
    for kvh in range(n_kv):
        ksl = slice(kvh * hd, (kvh + 1) * hd)
        k = jnp.concatenate([kp_ref[:, ksl], kc_ref[:, ksl], kn_ref[:, ksl]], axis=0)
        v = jnp.concatenate([vp_ref[:, ksl], vc_ref[:, ksl], vn_ref[:, ksl]], axis=0)
        v_aug = jnp.concatenate([jnp.concatenate([v, ones_blk], axis=1), sink_v], axis=0)
        heads = range(kvh * group, (kvh + 1) * group)
        q = jnp.concatenate([q_ref[:, h * hd:(h + 1) * hd] for h in heads], axis=0)
        st = lax.dot_general(k, q, (((1,), (1,)), ((), ())), preferred_element_type=F32)
        pts = []
        for gi, h in enumerate(heads):
            logit = st[:, gi * blk:(gi + 1) * blk] - slope_ref[h] * distm
            sk = sink_ref[h]
            m = jnp.maximum(jnp.max(logit, axis=0, keepdims=True), sk)
            p = jnp.exp2(logit - m)
            sink_p = jnp.where(first_q, jnp.exp2(sk - m), 0.0)
            pts.append(jnp.concatenate([p, sink_p], axis=0).astype(BF16))
        oa = lax.dot_general(jnp.concatenate(pts, axis=1), v_aug, (((0,), (0,)), ((), ())),
                             preferred_element_type=F32)
        for gi, h in enumerate(heads):
            rows = slice(gi * blk, (gi + 1) * blk)
            o = oa[rows, :hd] * (1.0 / oa[rows, hd:])
            g_half = glo_ref if h < heads_per_g else ghi_ref
            hg = h % heads_per_g
            gate = _silu(g_half[:, hg * hd:(hg + 1) * hd].astype(F32))
            o_ref[:, h * hd:(h + 1) * hd] = (o * gate).astype(o_ref.dtype)


def _attention(proj, slopes, sink, n_heads):
    bsz, l, _ = proj.shape
    hd, blk = HEAD_DIM, WINDOW
    group = n_heads // N_KV_HEADS
    aw = n_heads * hd
    kvw = N_KV_HEADS * hd
    nb = l // blk
    k0 = aw // kvw
    v0 = k0 + 1
    gw = aw // 2
    g0 = (aw + 2 * kvw) // gw
    smem = pl.BlockSpec(memory_space=pltpu.SMEM)
    prev = lambda n: jnp.maximum(n - 1, 0)
    nxt = lambda n: jnp.minimum(n + 1, nb - 1)
    same = lambda n: n
    kv = lambda c0, f: pl.BlockSpec((None, blk, kvw), lambda b, n: (b, f(n), c0))
    gspec = lambda c0: pl.BlockSpec((None, blk, gw), lambda b, n: (b, n, c0))
    assert (aw + 2 * kvw) % gw == 0 and aw % kvw == 0
    return pl.pallas_call(
        functools.partial(_attn_kernel, group=group),
        grid=(bsz, nb),
        in_specs=[smem, smem,
                  pl.BlockSpec((None, blk, aw), lambda b, n: (b, n, 0)),
                  kv(k0, prev), kv(k0, same), kv(k0, nxt),
                  kv(v0, prev), kv(v0, same), kv(v0, nxt),
                  gspec(g0), gspec(g0 + 1)],
        out_specs=pl.BlockSpec((None, blk, aw), lambda b, n: (b, n, 0)),
        out_shape=jax.ShapeDtypeStruct((bsz, l, aw), BF16),
        compiler_params=_cparams("parallel", "arbitrary"),
        name="swa_attention",
    )(slopes, sink, proj, proj, proj, proj, proj, proj, proj, proj, proj)


def _hyena_filter_spectrum(l, e, hy, tabs):
    n1 = FFT_N1
    n2 = 2 * l // n1
    k_un, ss = _filters(l, e, n2, hy["w_f1"], hy["b_f1"], hy["fr1"], hy["w_f2"], hy["b_f2"],
                        hy["fr2"], hy["w_f3"], hy["b_f3"], hy["fr3"], hy["w_f4"])
    k6 = k_un.reshape(1, HY_ORDER, 1, n2, n1, e)
    are, aim = _fft_a(k6, 0, tabs["ta_filt"])
    return _fft_b_filter(are, aim, tabs["tb_fwd"], ss)


def _hyena_layer(x, hy, kf, tabs, ln_g, ln_b, alpha):
    bsz, l, d = x.shape
    e = hy["w_out"].shape[0]
    kre, kim = kf
    n1 = FFT_N1
    n2 = 2 * l // n1
    h1 = n1 // 2
    proj = _inproj(x.reshape(bsz * l, d), hy["w_in"], hy["b_in"], 4, permute=(l, n2))
    y = _hyena_core(proj.reshape(4, bsz // 2, 2, n2, h1, e), hy["w_sc"], hy["b_sc"], hy["h_bias"],
                    kre, kim, tabs["ta_pair"], tabs["tb_fwd_pair"], tabs["tb_inv_pair"],
                    tabs["tc"])
    out = _outproj_ln_permuted(y.reshape(bsz, n2, h1, e), hy["w_out"], hy["b_out"],
                               x.reshape(bsz, h1, n2, d), ln_g, ln_b, alpha)
    return out.reshape(bsz, l, d)


def _attention_layer(x, at, ln_g, ln_b, alpha):
    bsz, l, d = x.shape
    n_heads = at["sink"].shape[0]
    x2 = x.reshape(bsz * l, d)
    proj = _inproj(x2, at["w_in"], at["b_in"], 1, col_scale=at["col_scale"])
    proj = proj.reshape(bsz, l, -1)
    o = _attention(proj, at["slopes"], at["sink"] * LOG2E, n_heads)
    out = _outproj_ln(o.reshape(bsz * l, -1), at["w_out"], at["b_out"], x2, ln_g, ln_b, alpha)
    return out.reshape(bsz, l, d)


def kernel(x_prompt, x_sample, ln_g, ln_b, hy_w_in, hy_b_in, hy_w_sc, hy_b_sc, hy_w_f1, hy_b_f1,
           hy_fr1, hy_w_f2, hy_b_f2, hy_fr2, hy_w_f3, hy_b_f3, hy_fr3, hy_w_f4, hy_h_bias,
           hy_w_out, hy_b_out, at_w_in, at_sink, at_w_out):
    depth = ln_g.shape[0]
    alpha = (2 * depth) ** 0.25
    l = x_prompt.shape[1]
    d = x_prompt.shape[2]
    assert x_sample.shape[1] == l and (2 * l) % FFT_N1 == 0
    assert x_prompt.shape[0] % 2 == 0 and x_sample.shape[0] % 2 == 0
    tabs = {k: jnp.asarray(t).astype(BF16) for k, t in _dft_tables(2 * l).items()}
    row = lambda a: a.reshape(1, -1)

    layers = []
    for i in range(depth):
        j = i // N_MIXERS
        if i % N_MIXERS == 0:
            e = hy_w_out.shape[1]
            hy = dict(w_in=hy_w_in[j].astype(BF16), b_in=row(hy_b_in[j]), w_sc=hy_w_sc[j],
                      b_sc=row(hy_b_sc[j]), w_f1=hy_w_f1[j], b_f1=hy_b_f1[j], fr1=hy_fr1[j],
                      w_f2=hy_w_f2[j], b_f2=hy_b_f2[j], fr2=hy_fr2[j], w_f3=hy_w_f3[j],
                      b_f3=hy_b_f3[j], fr3=hy_fr3[j], w_f4=hy_w_f4[j], h_bias=hy_h_bias[j],
                      w_out=hy_w_out[j].astype(BF16), b_out=row(hy_b_out[j]))
            kf = _hyena_filter_spectrum(l, e, hy, tabs)
            layers.append(("hyena", hy, kf))
        else:
            n_heads = at_sink.shape[1]
            aw = n_heads * HEAD_DIM
            width = at_w_in.shape[2]
            col_scale = jnp.where(jnp.arange(width) < aw, HEAD_DIM ** -0.5 * LOG2E, 1.0)
            slopes = jnp.exp2(-8.0 * jnp.arange(1, n_heads + 1, dtype=F32) / n_heads) * LOG2E
            at = dict(w_in=at_w_in[j].astype(BF16), b_in=jnp.zeros((1, width), F32),
                      col_scale=row(col_scale.astype(F32)), slopes=slopes,
                      sink=at_sink[j], w_out=at_w_out[j].astype(BF16),
                      b_out=jnp.zeros((1, d), F32))
            layers.append(("attn", at, None))

    def trunk(x):
        for i, (kind, prm, kf) in enumerate(layers):
            g, b = row(ln_g[i]), row(ln_b[i])
            if kind == "hyena":
                x = _hyena_layer(x, prm, kf, tabs, g, b, alpha)
            else:
                x = _attention_layer(x, prm, g, b, alpha)
        return x

    return (trunk(x_prompt), trunk(x_sample))
```

```python
import functools
import math

import numpy as np
import jax
import jax.numpy as jnp
from jax import lax
from jax.experimental import pallas as pl
from jax.experimental.pallas import tpu as pltpu

F32 = jnp.float32
BF16 = jnp.bfloat16

N_MIXERS = 2
HY_ORDER = 2
HY_DIRS = 2
SHORT_CONV = 3
POS_EMB_DIM = 33
POS_BANDS = (POS_EMB_DIM - 1) // 2
DECAY_TARGET = 1e-2
FAST_DECAY_PCT = 0.3
SLOW_DECAY_PCT = 1.5
MIN_DECAY = math.log(DECAY_TARGET) / SLOW_DECAY_PCT
MAX_DECAY = math.log(DECAY_TARGET) / FAST_DECAY_PCT
N_KV_HEADS = 4
HEAD_DIM = 128
WINDOW = 128
LN_EPS = 1e-5
LOG2E = 1.4426950408889634
MASK_DIST = 1e30

FFT_N1 = 128
LANE = 128
BF16_ROWS = 16
F32_ROWS = 8
VMEM_LIMIT = 56 * 1024 * 1024


def _cparams(*sem, vmem_limit=VMEM_LIMIT):
    return pltpu.CompilerParams(dimension_semantics=sem, vmem_limit_bytes=vmem_limit)


def _swap_major_sublane(x):
    return jnp.swapaxes(x, 0, 1)


def _silu(x):
    return x * (0.5 * jnp.tanh(0.5 * x) + 0.5)


def _block(m):
    return np.block([[m.real, -m.imag], [m.imag, m.real]])


@functools.lru_cache(maxsize=None)
def _dft_tables(n):
    n1, n2 = FFT_N1, n // FFT_N1
    h1 = n1 // 2
    k1 = np.arange(n1)[:, None]
    ta_data, ta_filt = [], []
    for j in range(n2):
        m = np.exp(-2j * np.pi * (j * k1 / n + np.arange(n1)[None, :] * k1 / n1))
        ta_data.append(_block(m[:, :h1]))
        ta_filt.append(np.concatenate([m.real, m.imag], 0))
    f2 = np.exp(-2j * np.pi * np.arange(n2)[:, None] * np.arange(n2)[None, :] / n2)
    tb_fwd = _block(f2)
    t1 = np.arange(n2)[:, None]
    tb_inv = [_block(np.exp(2j * np.pi * (kk * t1 / n + np.arange(n2)[None, :] * t1 / n2)))
              for kk in range(n1)]
    hc = np.exp(2j * np.pi * np.arange(h1)[:, None] * np.arange(n1)[None, :] / n1) / n
    tc = _block(hc)
    f = lambda a: np.asarray(a, np.float32)
    pair = lambda t: np.concatenate([t[0::2], t[1::2]], axis=2)
    ta_data, tb_inv = np.stack(ta_data), np.stack(tb_inv)
    return dict(ta_filt=f(np.stack(ta_filt)), tb_fwd=f(tb_fwd), tc=f(tc),
                ta_pair=f(pair(ta_data)), tb_inv_pair=f(pair(tb_inv)),
                tb_fwd_pair=f(np.concatenate([tb_fwd, tb_fwd], axis=1)))


def _inproj_kernel(x_ref, w_ref, b_ref, cs_ref, o_ref, xb_ref, *, scaled):
    @pl.when(pl.program_id(1) == 0)
    def _():
        xb_ref[...] = x_ref[...].astype(BF16)

    acc = jnp.dot(xb_ref[...], w_ref[...], preferred_element_type=F32)
    if scaled:
        acc = acc * cs_ref[...]
    val = (acc + b_ref[...]).astype(o_ref.dtype)
    if len(o_ref.shape) == 3:
        n2, n1b, tn = o_ref.shape
        val = _swap_major_sublane(val.reshape(n1b, n2, tn))
    o_ref[...] = val


def _inproj(x, w, b, n_comp, col_scale=None, permute=None, tm=1024, tn=1024):
    m, k = x.shape
    n = w.shape[1]
    wc = n // n_comp
    tn = min(tn, wc)
    ncb = wc // tn
    scaled = col_scale is not None
    if not scaled:
        col_scale = b
    if permute is None:
        out_spec = pl.BlockSpec((None, tm, tn), lambda i, j: (j // ncb, i, j % ncb))
        out_shape = jax.ShapeDtypeStruct((n_comp, m, wc), BF16)
    else:
        l, n2 = permute
        tm = BF16_ROWS * n2
        tpb = l // tm
        out_spec = pl.BlockSpec((None, None, n2, BF16_ROWS, tn),
                                lambda i, j: (j // ncb, i // tpb, 0, i % tpb, j % ncb))
        out_shape = jax.ShapeDtypeStruct((n_comp, m // l, n2, l // n2, wc), BF16)
    vec = pl.BlockSpec((1, tn), lambda i, j: (0, j))
    return pl.pallas_call(
        functools.partial(_inproj_kernel, scaled=scaled),
        grid=(m // tm, n // tn),
        in_specs=[
            pl.BlockSpec((tm, k), lambda i, j: (i, 0)),
            pl.BlockSpec((k, tn), lambda i, j: (0, j)),
            vec, vec,
        ],
        out_specs=out_spec,
        out_shape=out_shape,
        scratch_shapes=[pltpu.VMEM((tm, k), BF16)],
        compiler_params=_cparams("parallel", "arbitrary"),
        name="inproj",
    )(x, w, b, col_scale)


def _outproj_ln_kernel(y_ref, w_ref, b_ref, x_ref, g_ref, beta_ref, o_ref, *, alpha, permuted):
    y = y_ref[...]
    if permuted:
        nb, na, kk = y.shape
        y = y.reshape(nb * na, kk)
    h = jnp.dot(y, w_ref[...], preferred_element_type=F32) + b_ref[...]
    if permuted:
        h = _swap_major_sublane(h.reshape(nb, na, h.shape[-1]))
    r = alpha * x_ref[...] + h
    mu = jnp.mean(r, axis=-1, keepdims=True)
    d = r - mu
    var = jnp.mean(d * d, axis=-1, keepdims=True)
    o_ref[...] = d * lax.rsqrt(var + LN_EPS) * g_ref[...] + beta_ref[...]


def _outproj_ln(y, w, b, x, g, beta, alpha, tm=512):
    m, k = y.shape
    d = w.shape[1]
    row = lambda i: (i, 0)
    fixed = lambda i: (0, 0)
    return pl.pallas_call(
        functools.partial(_outproj_ln_kernel, alpha=alpha, permuted=False),
        grid=(m // tm,),
        in_specs=[
            pl.BlockSpec((tm, k), row),
            pl.BlockSpec((k, d), fixed),
            pl.BlockSpec((1, d), fixed),
            pl.BlockSpec((tm, d), row),
            pl.BlockSpec((1, d), fixed),
            pl.BlockSpec((1, d), fixed),
        ],
        out_specs=pl.BlockSpec((tm, d), row),
        out_shape=jax.ShapeDtypeStruct((m, d), F32),
        compiler_params=_cparams("parallel"),
        name="outproj_ln",
    )(y, w, b, x, g, beta)


def _outproj_ln_permuted(y, w, b, x, g, beta, alpha):
    bsz, n2, h1, k = y.shape
    d = w.shape[1]
    nb = F32_ROWS
    fixed = lambda bi, j: (0, 0)
    xspec = pl.BlockSpec((None, h1, nb, d), lambda bi, j: (bi, 0, j, 0))
    return pl.pallas_call(
        functools.partial(_outproj_ln_kernel, alpha=alpha, permuted=True),
        grid=(bsz, n2 // nb),
        in_specs=[
            pl.BlockSpec((None, nb, h1, k), lambda bi, j: (bi, j, 0, 0)),
            pl.BlockSpec((k, d), fixed),
            pl.BlockSpec((1, d), fixed),
            xspec,
            pl.BlockSpec((1, d), fixed),
            pl.BlockSpec((1, d), fixed),
        ],
        out_specs=xspec,
        out_shape=jax.ShapeDtypeStruct((bsz, h1, n2, d), F32),
        compiler_params=_cparams("parallel", "parallel"),
        name="outproj_ln_perm",
    )(y, w, b, x, g, beta)


def _filter_kernel(feat_ref, w1_ref, b1_ref, fr1_ref, w2_ref, b2_ref, fr2_ref, w3_ref, b3_ref,
                   fr3_ref, w4_ref, delta_ref, k_ref, ss_ref, h_ref):
    half = pl.program_id(0)
    hp = lax.Precision.HIGHEST

    @pl.when((pl.program_id(1) == 0) & (pl.program_id(2) == 0))
    def _():
        h = jnp.sin(fr1_ref[...] * (jnp.dot(feat_ref[...], w1_ref[...], precision=hp,
                                             preferred_element_type=F32) + b1_ref[...]))
        h = jnp.sin(fr2_ref[...] * (jnp.dot(h, w2_ref[...], precision=hp,
                                             preferred_element_type=F32) + b2_ref[...]))
        h = jnp.sin(fr3_ref[...] * (jnp.dot(h, w3_ref[...], precision=hp,
                                             preferred_element_type=F32) + b3_ref[...]))
        h_hi = h.astype(BF16)
        h_lo = (h - h_hi.astype(F32)).astype(BF16)
        h_ref[...] = jnp.concatenate([h_hi, h_hi, h_lo], axis=1)

    w4 = w4_ref[...]
    w_hi = w4.astype(BF16)
    w_lo = (w4 - w_hi.astype(F32)).astype(BF16)
    k = jnp.dot(h_ref[...], jnp.concatenate([w_hi, w_lo, w_hi], axis=0), preferred_element_type=F32)
    t_norm = feat_ref[:, 0:1]
    k = k * jnp.exp(-t_norm * delta_ref[...])
    row = lax.broadcasted_iota(jnp.int32, k.shape, 0)
    k = jnp.where((row == 0) & (half == 1), 0.0, k)
    ss_ref[...] = jnp.sum(k * k, axis=0, keepdims=True)
    n2, rows, ec = k_ref.shape
    k_ref[...] = _swap_major_sublane(k.astype(k_ref.dtype).reshape(rows, n2, ec))


def _filters(l, e, n2, w_f1, b_f1, fr1, w_f2, b_f2, fr2, w_f3, b_f3, fr3, w_f4, ec=512):
    fh = w_f1.shape[1]
    ec = min(ec, e)
    t_norm = jnp.linspace(0.0, 1.0, l, dtype=F32)
    w = 2.0 * math.pi * jnp.arange(l, dtype=F32) / l
    f = jnp.linspace(1e-4, POS_BANDS - 1, POS_BANDS, dtype=F32)
    ang = w[:, None] * f[None, :]
    feat = jnp.concatenate([t_norm[:, None], jnp.cos(ang), -jnp.sin(ang)], axis=-1)
    feat_rev = jnp.concatenate([feat[:1], feat[:0:-1]], axis=0)
    feat2 = jnp.pad(jnp.concatenate([feat, feat_rev], axis=0), ((0, 0), (0, LANE - POS_EMB_DIM)))
    w1p = jnp.pad(w_f1, ((0, LANE - POS_EMB_DIM), (0, 0)))
    deltas = jnp.abs(jnp.linspace(MIN_DECAY, MAX_DECAY, e, dtype=F32))[None, :]
    nec = e // ec
    rows = l // n2
    vec = lambda a: a.reshape(1, -1)
    fixed = lambda h, o, j: (0, 0)
    return pl.pallas_call(
        _filter_kernel,
        grid=(HY_DIRS, HY_ORDER, nec),
        in_specs=[
            pl.BlockSpec((l, LANE), lambda h, o, j: (h, 0)),
            pl.BlockSpec((LANE, fh), fixed), pl.BlockSpec((1, fh), fixed), pl.BlockSpec((1, fh), fixed),
            pl.BlockSpec((fh, fh), fixed), pl.BlockSpec((1, fh), fixed), pl.BlockSpec((1, fh), fixed),
            pl.BlockSpec((fh, fh), fixed), pl.BlockSpec((1, fh), fixed), pl.BlockSpec((1, fh), fixed),
            pl.BlockSpec((fh, ec), lambda h, o, j: (0, (o * HY_DIRS + h) * nec + j)),
            pl.BlockSpec((1, ec), lambda h, o, j: (0, j)),
        ],
        out_specs=[
            pl.BlockSpec((None, n2, rows, ec), lambda h, o, j: (o, 0, h, j)),
            pl.BlockSpec((None, None, 1, ec), lambda h, o, j: (o, h, 0, j)),
        ],
        out_shape=[
            jax.ShapeDtypeStruct((HY_ORDER, n2, HY_DIRS * rows, e), BF16),
            jax.ShapeDtypeStruct((HY_ORDER, HY_DIRS, 1, e), F32),
        ],
        scratch_shapes=[pltpu.VMEM((l, 3 * fh), BF16)],
        compiler_params=_cparams("arbitrary", "arbitrary", "arbitrary"),
        name="hyena_filter",
    )(feat2, w1p, vec(b_f1), vec(fr1), w_f2, vec(b_f2), vec(fr2), w_f3, vec(b_f3), vec(fr3),
      w_f4, deltas)


FFT_G = BF16_ROWS
FFT_EC = 512


def _fft_a_kernel(z_ref, ta_ref, are_ref, aim_ref):
    g = ta_ref.shape[0]
    n1 = are_ref.shape[0]
    rs = []
    for j in range(g):
        z = z_ref[:, j].reshape(ta_ref.shape[2], z_ref.shape[-1])
        rs.append(jnp.dot(ta_ref[j], z, preferred_element_type=F32).astype(BF16))
    rt = _swap_major_sublane(jnp.stack(rs))
    are_ref[...] = rt[:n1]
    aim_ref[...] = rt[n1:]


def _fft_a(z6, comp, ta, ec=FFT_EC, g=FFT_G):
    _, p, s, n2, r, e = z6.shape
    _, m2, kk = ta.shape
    n1 = m2 // 2
    ec = min(ec, e)
    out = jax.ShapeDtypeStruct((p, n1, n2, e), BF16)
    ospec = pl.BlockSpec((None, n1, g, ec), lambda pi, j, c: (pi, 0, j, c))
    return pl.pallas_call(
        _fft_a_kernel,
        grid=(p, n2 // g, e // ec),
        in_specs=[
            pl.BlockSpec((None, None, s, g, r, ec), lambda pi, j, c: (comp, pi, 0, j, 0, c)),
            pl.BlockSpec((g, m2, kk), lambda pi, j, c: (j, 0, 0)),
        ],
        out_specs=[ospec, ospec],
        out_shape=[out, out],
        compiler_params=_cparams("parallel", "arbitrary", "arbitrary"),
        name="fft_a",
    )(z6, ta)


def _fft_bf_kernel(are_ref, aim_ref, tf_ref, ss_ref, kre_ref, kim_ref):
    g, n2, _ = are_ref.shape
    rs = lax.rsqrt(ss_ref[0] + ss_ref[1] + 1e-12)
    for j in range(g):
        a = jnp.concatenate([are_ref[j], aim_ref[j]], axis=0)
        x = jnp.dot(tf_ref[...], a, preferred_element_type=F32) * rs
        kre_ref[j] = x[:n2].astype(kre_ref.dtype)
        kim_ref[j] = x[n2:].astype(kim_ref.dtype)


def _fft_b_filter(are, aim, tf, ss, ec=FFT_EC, g=FFT_G):
    p, n1, n2, e = are.shape
    ec = min(ec, e)
    spec = pl.BlockSpec((None, g, n2, ec), lambda j, c, pi: (pi, j, 0, c))
    out = jax.ShapeDtypeStruct((p, n1, n2, e), BF16)
    return pl.pallas_call(
        _fft_bf_kernel,
        grid=(n1 // g, e // ec, p),
        in_specs=[spec, spec,
                  pl.BlockSpec((2 * n2, 2 * n2), lambda j, c, pi: (0, 0)),
                  pl.BlockSpec((None, HY_DIRS, 1, ec), lambda j, c, pi: (pi, 0, 0, c))],
        out_specs=[spec, spec],
        out_shape=[out, out],
        compiler_params=_cparams("parallel", "arbitrary", "arbitrary"),
        name="fft_b_filter",
    )(are, aim, tf, ss)


HC_EC = LANE
HC_G = BF16_ROWS
STAGE_B_SPLIT = 1
HC_VMEM_LIMIT = 60 * 1024 * 1024


def _blockdiag(a, b):
    z = jnp.zeros_like(a)
    return jnp.concatenate([jnp.concatenate([a, z], axis=1),
                            jnp.concatenate([z, b], axis=1)], axis=0)


def _hyena_core_kernel(p_ref, wsc_ref, bsc_ref, hb_ref, kre_ref, kim_ref, tap_ref, tfp_ref,
                       tgp_ref, tc_ref, y_ref, a_scr, c_scr, z_scr):
    _, two, n2, h1, ec = p_ref.shape
    n1, g = a_scr.shape[2], a_scr.shape[3]
    rows = two * h1
    ngrp_a, ngrp_b = n2 // g, n1 // g
    half = g // 2
    rowi = lax.broadcasted_iota(jnp.int32, (rows, ec), 0) & (h1 - 1)

    def slab(comp, idx):
        return p_ref[comp, :, idx].reshape(rows, ec).astype(F32)

    def conv_group(comp, gi):
        base = gi * g
        first, last = gi == 0, gi == ngrp_a - 1
        w = [wsc_ref[t, comp:comp + 1, :] for t in range(SHORT_CONV)]
        b = bsc_ref[comp:comp + 1, :]
        prev = slab(comp, jnp.where(first, n2 - 1, base - 1))
        prev = jnp.where(first, jnp.where(rowi == 0, 0.0, pltpu.roll(prev, 1, axis=0)), prev)
        cur = slab(comp, base)
        for j in range(g):
            if j < g - 1:
                nxt = slab(comp, base + j + 1)
            else:
                nxt = slab(comp, jnp.where(last, 0, base + g))
                nxt = jnp.where(last, jnp.where(rowi == h1 - 1, 0.0,
                                                pltpu.roll(nxt, rows - 1, axis=0)), nxt)
            yield b + w[0] * prev + w[1] * cur + w[2] * nxt
            prev, cur = cur, nxt

    def stage_a_group(gi, zs):
        outs = []
        for jj in range(half):
            r = jnp.dot(tap_ref[gi * half + jj], _blockdiag(zs[2 * jj], zs[2 * jj + 1]),
                        preferred_element_type=F32).astype(BF16)
            outs += [r[:, :ec], r[:, ec:]]
        rt = _swap_major_sublane(jnp.stack(outs))
        a_scr[0, gi] = rt[:n1]
        a_scr[1, gi] = rt[n1:]

    def stage_b(order):
        def col(k1):
            return jnp.concatenate([a_scr[0, :, k1].reshape(n2, ec),
                                    a_scr[1, :, k1].reshape(n2, ec)], axis=0)

        def body(kb, carry):
            cs = []
            sub = half // STAGE_B_SPLIT
            for s0 in range(0, half, sub):
                pairs = range(s0, s0 + sub)
                xs = [jnp.dot(tfp_ref[...], _blockdiag(col(kb * g + 2 * jj), col(kb * g + 2 * jj + 1)),
                              preferred_element_type=F32) for jj in pairs]
                rhs = []
                for x, jj in zip(xs, pairs):
                    ka = kb * g + 2 * jj
                    kr = jnp.concatenate([kre_ref[order, ka], kre_ref[order, ka + 1]], axis=1).astype(F32)
                    ki = jnp.concatenate([kim_ref[order, ka], kim_ref[order, ka + 1]], axis=1).astype(F32)
                    xr, xi = x[:n2], x[n2:]
                    yr = (xr * kr - xi * ki).astype(BF16)
                    yi = (xr * ki + xi * kr).astype(BF16)
                    rhs.append(_blockdiag(jnp.concatenate([yr[:, :ec], yi[:, :ec]], axis=0),
                                          jnp.concatenate([yr[:, ec:], yi[:, ec:]], axis=0)))
                for r, jj in zip(rhs, pairs):
                    c = jnp.dot(tgp_ref[kb * half + jj], r,
                                preferred_element_type=F32).astype(BF16)
                    cs += [c[:, :ec], c[:, ec:]]
            ct = _swap_major_sublane(jnp.stack(cs))
            c_scr[0, kb] = ct[:n2]
            c_scr[1, kb] = ct[n2:]
            return carry

        lax.fori_loop(0, ngrp_b, body, 0)

    def stage_c_slab(t1):
        c = jnp.concatenate([c_scr[0, :, t1].reshape(n1, ec), c_scr[1, :, t1].reshape(n1, ec)], axis=0)
        return jnp.dot(tc_ref[...], c, preferred_element_type=F32)

    def body_a1(gi, carry):
        zs = []
        for j, v in enumerate(conv_group(0, gi)):
            zs.append(v.astype(BF16))
            z_scr[gi * g + j] = zs[j]
        stage_a_group(gi, zs)
        return carry

    def body_c1(gi, carry):
        zs = []
        for j, gate in enumerate(conv_group(1, gi)):
            t1 = gi * g + j
            z1 = gate * (stage_c_slab(t1) + hb_ref[0:1, :] * z_scr[t1].astype(F32))
            zs.append(z1.astype(BF16))
            z_scr[t1] = zs[j]
        stage_a_group(gi, zs)
        return carry

    def body_c2(gi, carry):
        for j, gate in enumerate(conv_group(2, gi)):
            t1 = gi * g + j
            z2 = gate * (stage_c_slab(t1) + hb_ref[1:2, :] * z_scr[t1].astype(F32))
            out = z2 * _silu(slab(3, t1))
            y_ref[:, t1] = out.astype(BF16).reshape(two, h1, ec)
        return carry

    lax.fori_loop(0, ngrp_a, body_a1, 0)
    stage_b(0)
    lax.fori_loop(0, ngrp_a, body_c1, 0)
    stage_b(1)
    lax.fori_loop(0, ngrp_a, body_c2, 0)


def _hyena_core(p6, w_sc, b_sc, h_bias, kre, kim, tap, tfp, tgp, tc, ec=HC_EC, g=HC_G):
    _, p, two, n2, h1, e = p6.shape
    n1 = kre.shape[1]
    ec = min(ec, e)
    once = pl.Buffered(1)
    const = lambda shape: pl.BlockSpec(shape, lambda c, pi: (0,) * len(shape), pipeline_mode=once)
    kspec = pl.BlockSpec((HY_ORDER, n1, n2, ec), lambda c, pi: (0, 0, 0, c))
    return pl.pallas_call(
        _hyena_core_kernel,
        grid=(e // ec, p),
        in_specs=[
            pl.BlockSpec((4, None, two, n2, h1, ec), lambda c, pi: (0, pi, 0, 0, 0, c)),
            pl.BlockSpec((SHORT_CONV, 3, ec), lambda c, pi: (0, 0, c)),
            pl.BlockSpec((3, ec), lambda c, pi: (0, c)),
            pl.BlockSpec((HY_ORDER, ec), lambda c, pi: (0, c)),
            kspec, kspec,
            const(tap.shape), const(tfp.shape), const(tgp.shape), const(tc.shape),
        ],
        out_specs=pl.BlockSpec((None, two, n2, h1, ec), lambda c, pi: (pi, 0, 0, 0, c)),
        out_shape=jax.ShapeDtypeStruct((p, two, n2, h1, e), BF16),
        scratch_shapes=[pltpu.VMEM((2, n2 // g, n1, g, ec), BF16),
                        pltpu.VMEM((2, n1 // g, n2, g, ec), BF16),
                        pltpu.VMEM((n2, two * h1, ec), BF16)],
        compiler_params=_cparams("parallel", "arbitrary", vmem_limit=HC_VMEM_LIMIT),
        name="hyena_core",
    )(p6, w_sc.reshape(SHORT_CONV, 3, e), b_sc.reshape(3, e), h_bias, kre, kim, tap, tfp, tgp, tc)


ATT_QB = 2


def _attn_kernel(slope_ref, sink_ref, q_ref, *refs, group):
    nkb = ATT_QB + 2
    k_refs, v_refs = refs[:nkb], refs[nkb:2 * nkb]
    glo_ref, ghi_ref, o_ref = refs[2 * nkb:]
    for s in range(ATT_QB):
        rows = slice(s * WINDOW, (s + 1) * WINDOW)
        _attn_block(slope_ref, sink_ref, q_ref.at[rows], k_refs[s:s + 3], v_refs[s:s + 3],
                    glo_ref.at[rows], ghi_ref.at[rows], o_ref.at[rows],
                    pl.program_id(1) * ATT_QB + s, pl.num_programs(1) * ATT_QB, group)


def _attn_block(slope_ref, sink_ref, q_ref, k_refs, v_refs, glo_ref, ghi_ref, o_ref, n, nb, group):
    kp_ref, kc_ref, kn_ref = k_refs
    vp_ref, vc_ref, vn_ref = v_refs
    blk = q_ref.shape[0]
    hd = HEAD_DIM
    n_kv = kc_ref.shape[1] // hd
    heads_per_g = glo_ref.shape[1] // hd
    nk = 3 * blk
    sj = lax.broadcasted_iota(jnp.int32, (nk, blk), 0)
    qi = lax.broadcasted_iota(jnp.int32, (nk, blk), 1)
    dist = jnp.abs(qi - sj + blk)
    valid = (dist <= WINDOW) & ((sj >= blk) | (n > 0)) & ((sj < 2 * blk) | (n < nb - 1))
    distm = jnp.where(valid, dist.astype(F32), MASK_DIST)
    pad = BF16_ROWS
    first_q = lax.broadcasted_iota(jnp.int32, (pad, blk), 0) == 0
    first_d = lax.broadcasted_iota(jnp.int32, (pad, hd), 0) == 0
    ones_blk = jnp.ones((nk, hd), BF16)
    sink_v = jnp.concatenate([jnp.zeros((pad, hd), BF16),
                              jnp.where(first_d, 1.0, 0.0).astype(BF16)], axis=1)
    for kvh in range(n_kv):
        ksl = slice(kvh * hd, (kvh + 1) * hd)
        k = jnp.concatenate([kp_ref[:, ksl], kc_ref[:, ksl], kn_ref[:, ksl]], axis=0)
        v = jnp.concatenate([vp_ref[:, ksl], vc_ref[:, ksl], vn_ref[:, ksl]], axis=0)
        v_aug = jnp.concatenate([jnp.concatenate([v, ones_blk], axis=1), sink_v], axis=0)
        heads = range(kvh * group, (kvh + 1) * group)
        q = jnp.concatenate([q_ref[:, h * hd:(h + 1) * hd] for h in heads], axis=0)
        st = lax.dot_general(k, q, (((1,), (1,)), ((), ())), preferred_element_type=F32)
        pts = []
        for gi, h in enumerate(heads):
            logit = st[:, gi * blk:(gi + 1) * blk] - slope_ref[h] * distm
            sk = sink_ref[h]
            m = jnp.maximum(jnp.max(logit, axis=0, keepdims=True), sk)
            p = jnp.exp2(logit - m)
            sink_p = jnp.where(first_q, jnp.exp2(sk - m), 0.0)
            pts.append(jnp.concatenate([p, sink_p], axis=0).astype(BF16))
        oa = lax.dot_general(jnp.concatenate(pts, axis=1), v_aug, (((0,), (0,)), ((), ())),
                             preferred_element_type=F32)
        for gi, h in enumerate(heads):
            rows = slice(gi * blk, (gi + 1) * blk)
            o = oa[rows, :hd] * (1.0 / oa[rows, hd:])
            g_half = glo_ref if h < heads_per_g else ghi_ref
            hg = h % heads_per_g
            gate = _silu(g_half[:, hg * hd:(hg + 1) * hd].astype(F32))
            o_ref[:, h * hd:(h + 1) * hd] = (o * gate).astype(o_ref.dtype)


def _attention(proj, slopes, sink, n_heads):
    bsz, l, _ = proj.shape
    hd, blk = HEAD_DIM, WINDOW
    group = n_heads // N_KV_HEADS
    aw = n_heads * hd
    kvw = N_KV_HEADS * hd
    nb = l // blk
    k0 = aw // kvw
    v0 = k0 + 1
    gw = aw // 2
    g0 = (aw + 2 * kvw) // gw
    smem = pl.BlockSpec(memory_space=pltpu.SMEM)
    qb = ATT_QB
    kv = lambda c0, off: pl.BlockSpec(
        (None, blk, kvw), lambda b, n: (b, jnp.clip(qb * n + off, 0, nb - 1), c0))
    kvs = lambda c0: [kv(c0, off) for off in range(-1, qb + 1)]
    gspec = lambda c0: pl.BlockSpec((None, qb * blk, gw), lambda b, n: (b, n, c0))
    assert (aw + 2 * kvw) % gw == 0 and aw % kvw == 0 and nb % qb == 0
    n_in = 3 + 2 * (qb + 2) + 2
    return pl.pallas_call(
        functools.partial(_attn_kernel, group=group),
        grid=(bsz, nb // qb),
        in_specs=[smem, smem,
                  pl.BlockSpec((None, qb * blk, aw), lambda b, n: (b, n, 0)),
                  *kvs(k0), *kvs(v0), gspec(g0), gspec(g0 + 1)],
        out_specs=pl.BlockSpec((None, qb * blk, aw), lambda b, n: (b, n, 0)),
        out_shape=jax.ShapeDtypeStruct((bsz, l, aw), BF16),
        compiler_params=_cparams("parallel", "arbitrary"),
        name="swa_attention",
    )(slopes, sink, *([proj] * (n_in - 2)))


def _hyena_filter_spectrum(l, e, hy, tabs):
    n1 = FFT_N1
    n2 = 2 * l // n1
    k_un, ss = _filters(l, e, n2, hy["w_f1"], hy["b_f1"], hy["fr1"], hy["w_f2"], hy["b_f2"],
                        hy["fr2"], hy["w_f3"], hy["b_f3"], hy["fr3"], hy["w_f4"])
    k6 = k_un.reshape(1, HY_ORDER, 1, n2, n1, e)
    are, aim = _fft_a(k6, 0, tabs["ta_filt"], ec=2 * FFT_EC)
    return _fft_b_filter(are, aim, tabs["tb_fwd"], ss, ec=4 * FFT_EC)


def _hyena_layer(x, hy, kf, tabs, ln_g, ln_b, alpha):
    bsz, l, d = x.shape
    e = hy["w_out"].shape[0]
    kre, kim = kf
    n1 = FFT_N1
    n2 = 2 * l // n1
    h1 = n1 // 2
    proj = _inproj(x.reshape(bsz * l, d), hy["w_in"], hy["b_in"], 4, permute=(l, n2))
    y = _hyena_core(proj.reshape(4, bsz // 2, 2, n2, h1, e), hy["w_sc"], hy["b_sc"], hy["h_bias"],
                    kre, kim, tabs["ta_pair"], tabs["tb_fwd_pair"], tabs["tb_inv_pair"],
                    tabs["tc"])
    out = _outproj_ln_permuted(y.reshape(bsz, n2, h1, e), hy["w_out"], hy["b_out"],
                               x.reshape(bsz, h1, n2, d), ln_g, ln_b, alpha)
    return out.reshape(bsz, l, d)


def _attention_layer(x, at, ln_g, ln_b, alpha):
    bsz, l, d = x.shape
    n_heads = at["sink"].shape[0]
    x2 = x.reshape(bsz * l, d)
    proj = _inproj(x2, at["w_in"], at["b_in"], 1, col_scale=at["col_scale"])
    proj = proj.reshape(bsz, l, -1)
    o = _attention(proj, at["slopes"], at["sink"] * LOG2E, n_heads)
    out = _outproj_ln(o.reshape(bsz * l, -1), at["w_out"], at["b_out"], x2, ln_g, ln_b, alpha)
    return out.reshape(bsz, l, d)


def kernel(x_prompt, x_sample, ln_g, ln_b, hy_w_in, hy_b_in, hy_w_sc, hy_b_sc, hy_w_f1, hy_b_f1,
           hy_fr1, hy_w_f2, hy_b_f2, hy_fr2, hy_w_f3, hy_b_f3, hy_fr3, hy_w_f4, hy_h_bias,
           hy_w_out, hy_b_out, at_w_in, at_sink, at_w_out):
    depth = ln_g.shape[0]
    alpha = (2 * depth) ** 0.25
    l = x_prompt.shape[1]
    d = x_prompt.shape[2]
    assert x_sample.shape[1] == l and (2 * l) % FFT_N1 == 0
    assert x_prompt.shape[0] % 2 == 0 and x_sample.shape[0] % 2 == 0
    tabs = {k: jnp.asarray(t).astype(BF16) for k, t in _dft_tables(2 * l).items()}
    row = lambda a: a.reshape(1, -1)

    layers = []
    for i in range(depth):
        j = i // N_MIXERS
        if i % N_MIXERS == 0:
            e = hy_w_out.shape[1]
            hy = dict(w_in=hy_w_in[j].astype(BF16), b_in=row(hy_b_in[j]), w_sc=hy_w_sc[j],
                      b_sc=row(hy_b_sc[j]), w_f1=hy_w_f1[j], b_f1=hy_b_f1[j], fr1=hy_fr1[j],
                      w_f2=hy_w_f2[j], b_f2=hy_b_f2[j], fr2=hy_fr2[j], w_f3=hy_w_f3[j],
                      b_f3=hy_b_f3[j], fr3=hy_fr3[j], w_f4=hy_w_f4[j], h_bias=hy_h_bias[j],
                      w_out=hy_w_out[j].astype(BF16), b_out=row(hy_b_out[j]))
            kf = _hyena_filter_spectrum(l, e, hy, tabs)
            layers.append(("hyena", hy, kf))
        else:
            n_heads = at_sink.shape[1]
            aw = n_heads * HEAD_DIM
            width = at_w_in.shape[2]
            col_scale = jnp.where(jnp.arange(width) < aw, HEAD_DIM ** -0.5 * LOG2E, 1.0)
            slopes = jnp.exp2(-8.0 * jnp.arange(1, n_heads + 1, dtype=F32) / n_heads) * LOG2E
            at = dict(w_in=at_w_in[j].astype(BF16), b_in=jnp.zeros((1, width), F32),
                      col_scale=row(col_scale.astype(F32)), slopes=slopes,
                      sink=at_sink[j], w_out=at_w_out[j].astype(BF16),
                      b_out=jnp.zeros((1, d), F32))
            layers.append(("attn", at, None))

    def trunk(x):
        for i, (kind, prm, kf) in enumerate(layers):
            g, b = row(ln_g[i]), row(ln_b[i])
            if kind == "hyena":
                x = _hyena_layer(x, prm, kf, tabs, g, b, alpha)
            else:
                x = _attention_layer(x, prm, g, b, alpha)
        return x

    return (trunk(x_prompt), trunk(x_sample))
```

```python
import functools
import math

import numpy as np
import jax
import jax.numpy as jnp
from jax import lax
from jax.experimental import pallas as pl
from jax.experimental.pallas import tpu as pltpu

F32 = jnp.float32
BF16 = jnp.bfloat16

N_MIXERS = 2
HY_ORDER = 2
HY_DIRS = 2
SHORT_CONV = 3
POS_EMB_DIM = 33
POS_BANDS = (POS_EMB_DIM - 1) // 2
DECAY_TARGET = 1e-2
FAST_DECAY_PCT = 0.3
SLOW_DECAY_PCT = 1.5
MIN_DECAY = math.log(DECAY_TARGET) / SLOW_DECAY_PCT
MAX_DECAY = math.log(DECAY_TARGET) / FAST_DECAY_PCT
N_KV_HEADS = 4
HEAD_DIM = 128
WINDOW = 128
LN_EPS = 1e-5
LOG2E = 1.4426950408889634
MASK_DIST = 1e30

FFT_N1 = 128
LANE = 128
BF16_ROWS = 16
F32_ROWS = 8
VMEM_LIMIT = 56 * 1024 * 1024


def _cparams(*sem, vmem_limit=VMEM_LIMIT):
    return pltpu.CompilerParams(dimension_semantics=sem, vmem_limit_bytes=vmem_limit)


def _swap_major_sublane(x):
    return jnp.swapaxes(x, 0, 1)


def _silu(x):
    return x * (0.5 * jnp.tanh(0.5 * x) + 0.5)


def _block(m):
    return np.block([[m.real, -m.imag], [m.imag, m.real]])


@functools.lru_cache(maxsize=None)
def _dft_tables(n):
    n1, n2 = FFT_N1, n // FFT_N1
    h1 = n1 // 2
    k1 = np.arange(n1)[:, None]
    ta_data, ta_filt = [], []
    for j in range(n2):
        m = np.exp(-2j * np.pi * (j * k1 / n + np.arange(n1)[None, :] * k1 / n1))
        ta_data.append(_block(m[:, :h1]))
        ta_filt.append(np.concatenate([m.real, m.imag], 0))
    f2 = np.exp(-2j * np.pi * np.arange(n2)[:, None] * np.arange(n2)[None, :] / n2)
    tb_fwd = _block(f2)
    t1 = np.arange(n2)[:, None]
    tb_inv = [_block(np.exp(2j * np.pi * (kk * t1 / n + np.arange(n2)[None, :] * t1 / n2)))
              for kk in range(n1)]
    hc = np.exp(2j * np.pi * np.arange(h1)[:, None] * np.arange(n1)[None, :] / n1) / n
    tc = _block(hc)
    f = lambda a: np.asarray(a, np.float32)
    pair = lambda t: np.concatenate([t[0::2], t[1::2]], axis=2)
    ta_data, tb_inv = np.stack(ta_data), np.stack(tb_inv)
    return dict(ta_filt=f(np.stack(ta_filt)), tb_fwd=f(tb_fwd), tc=f(tc),
                ta_pair=f(pair(ta_data)), tb_inv_pair=f(pair(tb_inv)),
                tb_fwd_pair=f(np.concatenate([tb_fwd, tb_fwd], axis=1)))


INPROJ_CHUNK = 1024


def _inproj_kernel(x_ref, w_ref, b_ref, cs_ref, o_ref, xb_ref, *, scaled):
    @pl.when(pl.program_id(1) == 0)
    def _():
        xb_ref[...] = x_ref[...].astype(BF16)

    tn = w_ref.shape[1]
    cw = min(tn, INPROJ_CHUNK)
    for c in range(tn // cw):
        sl = slice(c * cw, (c + 1) * cw)
        acc = jnp.dot(xb_ref[...], w_ref[:, sl], preferred_element_type=F32)
        if scaled:
            acc = acc * cs_ref[:, sl]
        val = (acc + b_ref[:, sl]).astype(o_ref.dtype)
        if len(o_ref.shape) == 3:
            n2, n1b, _ = o_ref.shape
            o_ref[:, :, sl] = _swap_major_sublane(val.reshape(n1b, n2, cw))
        else:
            o_ref[:, sl] = val


def _inproj(x, w, b, n_comp, col_scale=None, permute=None, tm=1024, tn=1024):
    m, k = x.shape
    n = w.shape[1]
    wc = n // n_comp
    tn = min(tn, wc)
    ncb = wc // tn
    scaled = col_scale is not None
    if not scaled:
        col_scale = b
    if permute is None:
        out_spec = pl.BlockSpec((None, tm, tn), lambda i, j: (j // ncb, i, j % ncb))
        out_shape = jax.ShapeDtypeStruct((n_comp, m, wc), BF16)
    else:
        l, n2 = permute
        tm = BF16_ROWS * n2
        tpb = l // tm
        out_spec = pl.BlockSpec((None, None, n2, BF16_ROWS, tn),
                                lambda i, j: (j // ncb, i // tpb, 0, i % tpb, j % ncb))
        out_shape = jax.ShapeDtypeStruct((n_comp, m // l, n2, l // n2, wc), BF16)
    vec = pl.BlockSpec((1, tn), lambda i, j: (0, j))
    return pl.pallas_call(
        functools.partial(_inproj_kernel, scaled=scaled),
        grid=(m // tm, n // tn),
        in_specs=[
            pl.BlockSpec((tm, k), lambda i, j: (i, 0)),
            pl.BlockSpec((k, tn), lambda i, j: (0, j)),
            vec, vec,
        ],
        out_specs=out_spec,
        out_shape=out_shape,
        scratch_shapes=[pltpu.VMEM((tm, k), BF16)],
        compiler_params=_cparams("parallel", "arbitrary"),
        name="inproj",
    )(x, w, b, col_scale)


def _outproj_ln_kernel(y_ref, w_ref, b_ref, x_ref, g_ref, beta_ref, o_ref, *, alpha, permuted):
    y = y_ref[...]
    if permuted:
        nb, na, kk = y.shape
        y = y.reshape(nb * na, kk)
    h = jnp.dot(y, w_ref[...], preferred_element_type=F32) + b_ref[...]
    if permuted:
        h = _swap_major_sublane(h.reshape(nb, na, h.shape[-1]))
    r = alpha * x_ref[...] + h
    mu = jnp.mean(r, axis=-1, keepdims=True)
    d = r - mu
    var = jnp.mean(d * d, axis=-1, keepdims=True)
    o_ref[...] = d * lax.rsqrt(var + LN_EPS) * g_ref[...] + beta_ref[...]


def _outproj_ln(y, w, b, x, g, beta, alpha, tm=512):
    m, k = y.shape
    d = w.shape[1]
    row = lambda i: (i, 0)
    fixed = lambda i: (0, 0)
    return pl.pallas_call(
        functools.partial(_outproj_ln_kernel, alpha=alpha, permuted=False),
        grid=(m // tm,),
        in_specs=[
            pl.BlockSpec((tm, k), row),
            pl.BlockSpec((k, d), fixed),
            pl.BlockSpec((1, d), fixed),
            pl.BlockSpec((tm, d), row),
            pl.BlockSpec((1, d), fixed),
            pl.BlockSpec((1, d), fixed),
        ],
        out_specs=pl.BlockSpec((tm, d), row),
        out_shape=jax.ShapeDtypeStruct((m, d), F32),
        compiler_params=_cparams("parallel"),
        name="outproj_ln",
    )(y, w, b, x, g, beta)


def _outproj_ln_permuted(y, w, b, x, g, beta, alpha):
    bsz, n2, h1, k = y.shape
    d = w.shape[1]
    nb = F32_ROWS
    fixed = lambda bi, j: (0, 0)
    xspec = pl.BlockSpec((None, h1, nb, d), lambda bi, j: (bi, 0, j, 0))
    return pl.pallas_call(
        functools.partial(_outproj_ln_kernel, alpha=alpha, permuted=True),
        grid=(bsz, n2 // nb),
        in_specs=[
            pl.BlockSpec((None, nb, h1, k), lambda bi, j: (bi, j, 0, 0)),
            pl.BlockSpec((k, d), fixed),
            pl.BlockSpec((1, d), fixed),
            xspec,
            pl.BlockSpec((1, d), fixed),
            pl.BlockSpec((1, d), fixed),
        ],
        out_specs=xspec,
        out_shape=jax.ShapeDtypeStruct((bsz, h1, n2, d), F32),
        compiler_params=_cparams("parallel", "parallel"),
        name="outproj_ln_perm",
    )(y, w, b, x, g, beta)


def _filter_kernel(feat_ref, w1_ref, b1_ref, fr1_ref, w2_ref, b2_ref, fr2_ref, w3_ref, b3_ref,
                   fr3_ref, w4_ref, delta_ref, k_ref, ss_ref, h_ref):
    half = pl.program_id(0)
    hp = lax.Precision.HIGHEST

    @pl.when((pl.program_id(1) == 0) & (pl.program_id(2) == 0))
    def _():
        h = jnp.sin(fr1_ref[...] * (jnp.dot(feat_ref[...], w1_ref[...], precision=hp,
                                             preferred_element_type=F32) + b1_ref[...]))
        h = jnp.sin(fr2_ref[...] * (jnp.dot(h, w2_ref[...], precision=hp,
                                             preferred_element_type=F32) + b2_ref[...]))
        h = jnp.sin(fr3_ref[...] * (jnp.dot(h, w3_ref[...], precision=hp,
                                             preferred_element_type=F32) + b3_ref[...]))
        h_hi = h.astype(BF16)
        h_lo = (h - h_hi.astype(F32)).astype(BF16)
        h_ref[...] = jnp.concatenate([h_hi, h_hi, h_lo], axis=1)

    w4 = w4_ref[...]
    w_hi = w4.astype(BF16)
    w_lo = (w4 - w_hi.astype(F32)).astype(BF16)
    k = jnp.dot(h_ref[...], jnp.concatenate([w_hi, w_lo, w_hi], axis=0), preferred_element_type=F32)
    t_norm = feat_ref[:, 0:1]
    k = k * jnp.exp(-t_norm * delta_ref[...])
    row = lax.broadcasted_iota(jnp.int32, k.shape, 0)
    k = jnp.where((row == 0) & (half == 1), 0.0, k)
    ss_ref[...] = jnp.sum(k * k, axis=0, keepdims=True)
    n2, rows, ec = k_ref.shape
    k_ref[...] = _swap_major_sublane(k.astype(k_ref.dtype).reshape(rows, n2, ec))


def _filters(l, e, n2, w_f1, b_f1, fr1, w_f2, b_f2, fr2, w_f3, b_f3, fr3, w_f4, ec=512):
    fh = w_f1.shape[1]
    ec = min(ec, e)
    t_norm = jnp.linspace(0.0, 1.0, l, dtype=F32)
    w = 2.0 * math.pi * jnp.arange(l, dtype=F32) / l
    f = jnp.linspace(1e-4, POS_BANDS - 1, POS_BANDS, dtype=F32)
    ang = w[:, None] * f[None, :]
    feat = jnp.concatenate([t_norm[:, None], jnp.cos(ang), -jnp.sin(ang)], axis=-1)
    feat_rev = jnp.concatenate([feat[:1], feat[:0:-1]], axis=0)
    feat2 = jnp.pad(jnp.concatenate([feat, feat_rev], axis=0), ((0, 0), (0, LANE - POS_EMB_DIM)))
    w1p = jnp.pad(w_f1, ((0, LANE - POS_EMB_DIM), (0, 0)))
    deltas = jnp.abs(jnp.linspace(MIN_DECAY, MAX_DECAY, e, dtype=F32))[None, :]
    nec = e // ec
    rows = l // n2
    vec = lambda a: a.reshape(1, -1)
    fixed = lambda h, o, j: (0, 0)
    return pl.pallas_call(
        _filter_kernel,
        grid=(HY_DIRS, HY_ORDER, nec),
        in_specs=[
            pl.BlockSpec((l, LANE), lambda h, o, j: (h, 0)),
            pl.BlockSpec((LANE, fh), fixed), pl.BlockSpec((1, fh), fixed), pl.BlockSpec((1, fh), fixed),
            pl.BlockSpec((fh, fh), fixed), pl.BlockSpec((1, fh), fixed), pl.BlockSpec((1, fh), fixed),
            pl.BlockSpec((fh, fh), fixed), pl.BlockSpec((1, fh), fixed), pl.BlockSpec((1, fh), fixed),
            pl.BlockSpec((fh, ec), lambda h, o, j: (0, (o * HY_DIRS + h) * nec + j)),
            pl.BlockSpec((1, ec), lambda h, o, j: (0, j)),
        ],
        out_specs=[
            pl.BlockSpec((None, n2, rows, ec), lambda h, o, j: (o, 0, h, j)),
            pl.BlockSpec((None, None, 1, ec), lambda h, o, j: (o, h, 0, j)),
        ],
        out_shape=[
            jax.ShapeDtypeStruct((HY_ORDER, n2, HY_DIRS * rows, e), BF16),
            jax.ShapeDtypeStruct((HY_ORDER, HY_DIRS, 1, e), F32),
        ],
        scratch_shapes=[pltpu.VMEM((l, 3 * fh), BF16)],
        compiler_params=_cparams("arbitrary", "arbitrary", "arbitrary"),
        name="hyena_filter",
    )(feat2, w1p, vec(b_f1), vec(fr1), w_f2, vec(b_f2), vec(fr2), w_f3, vec(b_f3), vec(fr3),
      w_f4, deltas)


FFT_G = BF16_ROWS
FFT_EC = 512


def _fft_a_kernel(z_ref, ta_ref, are_ref, aim_ref):
    g = ta_ref.shape[0]
    n1 = are_ref.shape[0]
    rs = []
    for j in range(g):
        z = z_ref[:, j].reshape(ta_ref.shape[2], z_ref.shape[-1])
        rs.append(jnp.dot(ta_ref[j], z, preferred_element_type=F32).astype(BF16))
    rt = _swap_major_sublane(jnp.stack(rs))
    are_ref[...] = rt[:n1]
    aim_ref[...] = rt[n1:]


def _fft_a(z6, comp, ta, ec=FFT_EC, g=FFT_G):
    _, p, s, n2, r, e = z6.shape
    _, m2, kk = ta.shape
    n1 = m2 // 2
    ec = min(ec, e)
    out = jax.ShapeDtypeStruct((p, n1, n2, e), BF16)
    ospec = pl.BlockSpec((None, n1, g, ec), lambda pi, j, c: (pi, 0, j, c))
    return pl.pallas_call(
        _fft_a_kernel,
        grid=(p, n2 // g, e // ec),
        in_specs=[
            pl.BlockSpec((None, None, s, g, r, ec), lambda pi, j, c: (comp, pi, 0, j, 0, c)),
            pl.BlockSpec((g, m2, kk), lambda pi, j, c: (j, 0, 0)),
        ],
        out_specs=[ospec, ospec],
        out_shape=[out, out],
        compiler_params=_cparams("parallel", "arbitrary", "arbitrary"),
        name="fft_a",
    )(z6, ta)


def _fft_bf_kernel(are_ref, aim_ref, tf_ref, ss_ref, kre_ref, kim_ref):
    g, n2, _ = are_ref.shape
    rs = lax.rsqrt(ss_ref[0] + ss_ref[1] + 1e-12)
    for j in range(g):
        a = jnp.concatenate([are_ref[j], aim_ref[j]], axis=0)
        x = jnp.dot(tf_ref[...], a, preferred_element_type=F32) * rs
        kre_ref[j] = x[:n2].astype(kre_ref.dtype)
        kim_ref[j] = x[n2:].astype(kim_ref.dtype)


def _fft_b_filter(are, aim, tf, ss, ec=FFT_EC, g=FFT_G):
    p, n1, n2, e = are.shape
    ec = min(ec, e)
    spec = pl.BlockSpec((None, g, n2, ec), lambda j, c, pi: (pi, j, 0, c))
    out = jax.ShapeDtypeStruct((p, n1, n2, e), BF16)
    return pl.pallas_call(
        _fft_bf_kernel,
        grid=(n1 // g, e // ec, p),
        in_specs=[spec, spec,
                  pl.BlockSpec((2 * n2, 2 * n2), lambda j, c, pi: (0, 0)),
                  pl.BlockSpec((None, HY_DIRS, 1, ec), lambda j, c, pi: (pi, 0, 0, c))],
        out_specs=[spec, spec],
        out_shape=[out, out],
        compiler_params=_cparams("parallel", "arbitrary", "arbitrary"),
        name="fft_b_filter",
    )(are, aim, tf, ss)


HC_EC = LANE
HC_G = BF16_ROWS
STAGE_B_SPLIT = 1
HC_VMEM_LIMIT = 60 * 1024 * 1024


def _blockdiag(a, b):
    z = jnp.zeros_like(a)
    return jnp.concatenate([jnp.concatenate([a, z], axis=1),
                            jnp.concatenate([z, b], axis=1)], axis=0)


def _hyena_core_kernel(p_ref, wsc_ref, bsc_ref, hb_ref, kre_ref, kim_ref, tap_ref, tfp_ref,
                       tgp_ref, tc_ref, y_ref, a_scr, c_scr, z_scr):
    _, two, n2, h1, ec = p_ref.shape
    n1, g = a_scr.shape[2], a_scr.shape[3]
    rows = two * h1
    ngrp_a, ngrp_b = n2 // g, n1 // g
    half = g // 2
    rowi = lax.broadcasted_iota(jnp.int32, (rows, ec), 0) & (h1 - 1)

    def slab(comp, idx):
        return p_ref[comp, :, idx].reshape(rows, ec).astype(F32)

    def conv_group(comp, gi):
        base = gi * g
        first, last = gi == 0, gi == ngrp_a - 1
        w = [wsc_ref[t, comp:comp + 1, :] for t in range(SHORT_CONV)]
        b = bsc_ref[comp:comp + 1, :]
        prev = slab(comp, jnp.where(first, n2 - 1, base - 1))
        prev = jnp.where(first, jnp.where(rowi == 0, 0.0, pltpu.roll(prev, 1, axis=0)), prev)
        cur = slab(comp, base)
        for j in range(g):
            if j < g - 1:
                nxt = slab(comp, base + j + 1)
            else:
                nxt = slab(comp, jnp.where(last, 0, base + g))
                nxt = jnp.where(last, jnp.where(rowi == h1 - 1, 0.0,
                                                pltpu.roll(nxt, rows - 1, axis=0)), nxt)
            yield b + w[0] * prev + w[1] * cur + w[2] * nxt
            prev, cur = cur, nxt

    def stage_a_group(gi, zs):
        outs = []
        for jj in range(half):
            r = jnp.dot(tap_ref[gi * half + jj], _blockdiag(zs[2 * jj], zs[2 * jj + 1]),
                        preferred_element_type=F32).astype(BF16)
            outs += [r[:, :ec], r[:, ec:]]
        rt = _swap_major_sublane(jnp.stack(outs))
        a_scr[0, gi] = rt[:n1]
        a_scr[1, gi] = rt[n1:]

    def stage_b(order):
        def col(k1):
            return jnp.concatenate([a_scr[0, :, k1].reshape(n2, ec),
                                    a_scr[1, :, k1].reshape(n2, ec)], axis=0)

        def body(kb, carry):
            cs = []
            sub = half // STAGE_B_SPLIT
            for s0 in range(0, half, sub):
                pairs = range(s0, s0 + sub)
                xs = [jnp.dot(tfp_ref[...], _blockdiag(col(kb * g + 2 * jj), col(kb * g + 2 * jj + 1)),
                              preferred_element_type=F32) for jj in pairs]
                rhs = []
                for x, jj in zip(xs, pairs):
                    ka = kb * g + 2 * jj
                    kr = jnp.concatenate([kre_ref[order, ka], kre_ref[order, ka + 1]], axis=1).astype(F32)
                    ki = jnp.concatenate([kim_ref[order, ka], kim_ref[order, ka + 1]], axis=1).astype(F32)
                    xr, xi = x[:n2], x[n2:]
                    yr = (xr * kr - xi * ki).astype(BF16)
                    yi = (xr * ki + xi * kr).astype(BF16)
                    rhs.append(_blockdiag(jnp.concatenate([yr[:, :ec], yi[:, :ec]], axis=0),
                                          jnp.concatenate([yr[:, ec:], yi[:, ec:]], axis=0)))
                for r, jj in zip(rhs, pairs):
                    c = jnp.dot(tgp_ref[kb * half + jj], r,
                                preferred_element_type=F32).astype(BF16)
                    cs += [c[:, :ec], c[:, ec:]]
            ct = _swap_major_sublane(jnp.stack(cs))
            c_scr[0, kb] = ct[:n2]
            c_scr[1, kb] = ct[n2:]
            return carry

        lax.fori_loop(0, ngrp_b, body, 0)

    def stage_c_slab(t1):
        c = jnp.concatenate([c_scr[0, :, t1].reshape(n1, ec), c_scr[1, :, t1].reshape(n1, ec)], axis=0)
        return jnp.dot(tc_ref[...], c, preferred_element_type=F32)

    def body_a1(gi, carry):
        zs = []
        for j, v in enumerate(conv_group(0, gi)):
            zs.append(v.astype(BF16))
            z_scr[gi * g + j] = zs[j]
        stage_a_group(gi, zs)
        return carry

    def body_c1(gi, carry):
        zs = []
        for j, gate in enumerate(conv_group(1, gi)):
            t1 = gi * g + j
            z1 = gate * (stage_c_slab(t1) + hb_ref[0:1, :] * z_scr[t1].astype(F32))
            zs.append(z1.astype(BF16))
            z_scr[t1] = zs[j]
        stage_a_group(gi, zs)
        return carry

    def body_c2(gi, carry):
        for j, gate in enumerate(conv_group(2, gi)):
            t1 = gi * g + j
            z2 = gate * (stage_c_slab(t1) + hb_ref[1:2, :] * z_scr[t1].astype(F32))
            out = z2 * _silu(slab(3, t1))
            y_ref[:, t1] = out.astype(BF16).reshape(two, h1, ec)
        return carry

    lax.fori_loop(0, ngrp_a, body_a1, 0)
    stage_b(0)
    lax.fori_loop(0, ngrp_a, body_c1, 0)
    stage_b(1)
    lax.fori_loop(0, ngrp_a, body_c2, 0)


def _hyena_core(p6, w_sc, b_sc, h_bias, kre, kim, tap, tfp, tgp, tc, ec=HC_EC, g=HC_G):
    _, p, two, n2, h1, e = p6.shape
    n1 = kre.shape[1]
    ec = min(ec, e)
    once = pl.Buffered(1)
    const = lambda shape: pl.BlockSpec(shape, lambda c, pi: (0,) * len(shape), pipeline_mode=once)
    kspec = pl.BlockSpec((HY_ORDER, n1, n2, ec), lambda c, pi: (0, 0, 0, c))
    return pl.pallas_call(
        _hyena_core_kernel,
        grid=(e // ec, p),
        in_specs=[
            pl.BlockSpec((4, None, two, n2, h1, ec), lambda c, pi: (0, pi, 0, 0, 0, c)),
            pl.BlockSpec((SHORT_CONV, 3, ec), lambda c, pi: (0, 0, c)),
            pl.BlockSpec((3, ec), lambda c, pi: (0, c)),
            pl.BlockSpec((HY_ORDER, ec), lambda c, pi: (0, c)),
            kspec, kspec,
            const(tap.shape), const(tfp.shape), const(tgp.shape), const(tc.shape),
        ],
        out_specs=pl.BlockSpec((None, two, n2, h1, ec), lambda c, pi: (pi, 0, 0, 0, c)),
        out_shape=jax.ShapeDtypeStruct((p, two, n2, h1, e), BF16),
        scratch_shapes=[pltpu.VMEM((2, n2 // g, n1, g, ec), BF16),
                        pltpu.VMEM((2, n1 // g, n2, g, ec), BF16),
                        pltpu.VMEM((n2, two * h1, ec), BF16)],
        compiler_params=_cparams("parallel", "arbitrary", vmem_limit=HC_VMEM_LIMIT),
        name="hyena_core",
    )(p6, w_sc.reshape(SHORT_CONV, 3, e), b_sc.reshape(3, e), h_bias, kre, kim, tap, tfp, tgp, tc)


ATT_QB = 8


def _attn_kernel(slope_ref, sink_ref, q_ref, *refs, group):
    nkb = ATT_QB + 2
    k_refs, v_refs = refs[:nkb], refs[nkb:2 * nkb]
    glo_ref, ghi_ref, o_ref = refs[2 * nkb:]
    for s in range(ATT_QB):
        rows = slice(s * WINDOW, (s + 1) * WINDOW)
        _attn_block(slope_ref, sink_ref, q_ref.at[rows], k_refs[s:s + 3], v_refs[s:s + 3],
                    glo_ref.at[rows], ghi_ref.at[rows], o_ref.at[rows],
                    pl.program_id(1) * ATT_QB + s, pl.num_programs(1) * ATT_QB, group)


def _attn_block(slope_ref, sink_ref, q_ref, k_refs, v_refs, glo_ref, ghi_ref, o_ref, n, nb, group):
    kp_ref, kc_ref, kn_ref = k_refs
    vp_ref, vc_ref, vn_ref = v_refs
    blk = q_ref.shape[0]
    hd = HEAD_DIM
    n_kv = kc_ref.shape[1] // hd
    heads_per_g = glo_ref.shape[1] // hd
    nk = 3 * blk
    sj = lax.broadcasted_iota(jnp.int32, (nk, blk), 0)
    qi = lax.broadcasted_iota(jnp.int32, (nk, blk), 1)
    dist = jnp.abs(qi - sj + blk)
    valid = (dist <= WINDOW) & ((sj >= blk) | (n > 0)) & ((sj < 2 * blk) | (n < nb - 1))
    distm = jnp.where(valid, dist.astype(F32), MASK_DIST)
    pad = BF16_ROWS
    first_q = lax.broadcasted_iota(jnp.int32, (pad, blk), 0) == 0
    first_d = lax.broadcasted_iota(jnp.int32, (pad, hd), 0) == 0
    ones_blk = jnp.ones((nk, hd), BF16)
    sink_v = jnp.concatenate([jnp.zeros((pad, hd), BF16),
                              jnp.where(first_d, 1.0, 0.0).astype(BF16)], axis=1)
    for kvh in range(n_kv):
        ksl = slice(kvh * hd, (kvh + 1) * hd)
        k = jnp.concatenate([kp_ref[:, ksl], kc_ref[:, ksl], kn_ref[:, ksl]], axis=0)
        v = jnp.concatenate([vp_ref[:, ksl], vc_ref[:, ksl], vn_ref[:, ksl]], axis=0)
        v_aug = jnp.concatenate([jnp.concatenate([v, ones_blk], axis=1), sink_v], axis=0)
        heads = range(kvh * group, (kvh + 1) * group)
        q = jnp.concatenate([q_ref[:, h * hd:(h + 1) * hd] for h in heads], axis=0)
        st = lax.dot_general(k, q, (((1,), (1,)), ((), ())), preferred_element_type=F32)
        pts = []
        for gi, h in enumerate(heads):
            logit = st[:, gi * blk:(gi + 1) * blk] - slope_ref[h] * distm
            sk = sink_ref[h]
            m = jnp.maximum(jnp.max(logit, axis=0, keepdims=True), sk)
            p = jnp.exp2(logit - m)
            sink_p = jnp.where(first_q, jnp.exp2(sk - m), 0.0)
            pts.append(jnp.concatenate([p, sink_p], axis=0).astype(BF16))
        oa = lax.dot_general(jnp.concatenate(pts, axis=1), v_aug, (((0,), (0,)), ((), ())),
                             preferred_element_type=F32)
        for gi, h in enumerate(heads):
            rows = slice(gi * blk, (gi + 1) * blk)
            o = oa[rows, :hd] * (1.0 / oa[rows, hd:])
            g_half = glo_ref if h < heads_per_g else ghi_ref
            hg = h % heads_per_g
            gate = _silu(g_half[:, hg * hd:(hg + 1) * hd].astype(F32))
            o_ref[:, h * hd:(h + 1) * hd] = (o * gate).astype(o_ref.dtype)


def _attention(proj, slopes, sink, n_heads):
    bsz, l, _ = proj.shape
    hd, blk = HEAD_DIM, WINDOW
    group = n_heads // N_KV_HEADS
    aw = n_heads * hd
    kvw = N_KV_HEADS * hd
    nb = l // blk
    k0 = aw // kvw
    v0 = k0 + 1
    gw = aw // 2
    g0 = (aw + 2 * kvw) // gw
    smem = pl.BlockSpec(memory_space=pltpu.SMEM)
    qb = ATT_QB
    kv = lambda c0, off: pl.BlockSpec(
        (None, blk, kvw), lambda b, n: (b, jnp.clip(qb * n + off, 0, nb - 1), c0))
    kvs = lambda c0: [kv(c0, off) for off in range(-1, qb + 1)]
    gspec = lambda c0: pl.BlockSpec((None, qb * blk, gw), lambda b, n: (b, n, c0))
    assert (aw + 2 * kvw) % gw == 0 and aw % kvw == 0 and nb % qb == 0
    n_in = 3 + 2 * (qb + 2) + 2
    return pl.pallas_call(
        functools.partial(_attn_kernel, group=group),
        grid=(bsz, nb // qb),
        in_specs=[smem, smem,
                  pl.BlockSpec((None, qb * blk, aw), lambda b, n: (b, n, 0)),
                  *kvs(k0), *kvs(v0), gspec(g0), gspec(g0 + 1)],
        out_specs=pl.BlockSpec((None, qb * blk, aw), lambda b, n: (b, n, 0)),
        out_shape=jax.ShapeDtypeStruct((bsz, l, aw), BF16),
        compiler_params=_cparams("parallel", "arbitrary"),
        name="swa_attention",
    )(slopes, sink, *([proj] * (n_in - 2)))


def _hyena_filter_spectrum(l, e, hy, tabs):
    n1 = FFT_N1
    n2 = 2 * l // n1
    k_un, ss = _filters(l, e, n2, hy["w_f1"], hy["b_f1"], hy["fr1"], hy["w_f2"], hy["b_f2"],
                        hy["fr2"], hy["w_f3"], hy["b_f3"], hy["fr3"], hy["w_f4"])
    k6 = k_un.reshape(1, HY_ORDER, 1, n2, n1, e)
    are, aim = _fft_a(k6, 0, tabs["ta_filt"], ec=2 * FFT_EC)
    return _fft_b_filter(are, aim, tabs["tb_fwd"], ss, ec=4 * FFT_EC)


def _hyena_layer(x, hy, kf, tabs, ln_g, ln_b, alpha):
    bsz, l, d = x.shape
    e = hy["w_out"].shape[0]
    kre, kim = kf
    n1 = FFT_N1
    n2 = 2 * l // n1
    h1 = n1 // 2
    proj = _inproj(x.reshape(bsz * l, d), hy["w_in"], hy["b_in"], 4, permute=(l, n2),
                   tn=2 * INPROJ_CHUNK)
    y = _hyena_core(proj.reshape(4, bsz // 2, 2, n2, h1, e), hy["w_sc"], hy["b_sc"], hy["h_bias"],
                    kre, kim, tabs["ta_pair"], tabs["tb_fwd_pair"], tabs["tb_inv_pair"],
                    tabs["tc"])
    out = _outproj_ln_permuted(y.reshape(bsz, n2, h1, e), hy["w_out"], hy["b_out"],
                               x.reshape(bsz, h1, n2, d), ln_g, ln_b, alpha)
    return out.reshape(bsz, l, d)


def _attention_layer(x, at, ln_g, ln_b, alpha):
    bsz, l, d = x.shape
    n_heads = at["sink"].shape[0]
    x2 = x.reshape(bsz * l, d)
    proj = _inproj(x2, at["w_in"], at["b_in"], 1, col_scale=at["col_scale"])
    proj = proj.reshape(bsz, l, -1)
    o = _attention(proj, at["slopes"], at["sink"] * LOG2E, n_heads)
    out = _outproj_ln(o.reshape(bsz * l, -1), at["w_out"], at["b_out"], x2, ln_g, ln_b, alpha)
    return out.reshape(bsz, l, d)


def kernel(x_prompt, x_sample, ln_g, ln_b, hy_w_in, hy_b_in, hy_w_sc, hy_b_sc, hy_w_f1, hy_b_f1,
           hy_fr1, hy_w_f2, hy_b_f2, hy_fr2, hy_w_f3, hy_b_f3, hy_fr3, hy_w_f4, hy_h_bias,
           hy_w_out, hy_b_out, at_w_in, at_sink, at_w_out):
    depth = ln_g.shape[0]
    alpha = (2 * depth) ** 0.25
    l = x_prompt.shape[1]
    d = x_prompt.shape[2]
    assert x_sample.shape[1] == l and (2 * l) % FFT_N1 == 0
    assert x_prompt.shape[0] % 2 == 0 and x_sample.shape[0] % 2 == 0
    tabs = {k: jnp.asarray(t).astype(BF16) for k, t in _dft_tables(2 * l).items()}
    row = lambda a: a.reshape(1, -1)

    layers = []
    for i in range(depth):
        j = i // N_MIXERS
        if i % N_MIXERS == 0:
            e = hy_w_out.shape[1]
            hy = dict(w_in=hy_w_in[j].astype(BF16), b_in=row(hy_b_in[j]), w_sc=hy_w_sc[j],
                      b_sc=row(hy_b_sc[j]), w_f1=hy_w_f1[j], b_f1=hy_b_f1[j], fr1=hy_fr1[j],
                      w_f2=hy_w_f2[j], b_f2=hy_b_f2[j], fr2=hy_fr2[j], w_f3=hy_w_f3[j],
                      b_f3=hy_b_f3[j], fr3=hy_fr3[j], w_f4=hy_w_f4[j], h_bias=hy_h_bias[j],
                      w_out=hy_w_out[j].astype(BF16), b_out=row(hy_b_out[j]))
            kf = _hyena_filter_spectrum(l, e, hy, tabs)
            layers.append(("hyena", hy, kf))
        else:
            n_heads = at_sink.shape[1]
            aw = n_heads * HEAD_DIM
            width = at_w_in.shape[2]
            col_scale = jnp.where(jnp.arange(width) < aw, HEAD_DIM ** -0.5 * LOG2E, 1.0)
            slopes = jnp.exp2(-8.0 * jnp.arange(1, n_heads + 1, dtype=F32) / n_heads) * LOG2E
            at = dict(w_in=at_w_in[j].astype(BF16), b_in=jnp.zeros((1, width), F32),
                      col_scale=row(col_scale.astype(F32)), slopes=slopes,
                      sink=at_sink[j], w_out=at_w_out[j].astype(BF16),
                      b_out=jnp.zeros((1, d), F32))
            layers.append(("attn", at, None))

    def trunk(x):
        for i, (kind, prm, kf) in enumerate(layers):
            g, b = row(ln_g[i]), row(ln_b[i])
            if kind == "hyena":
                x = _hyena_layer(x, prm, kf, tabs, g, b, alpha)
            else:
                x = _attention_layer(x, prm, g, b, alpha)
        return x

    return (trunk(x_prompt), trunk(x_sample))
```

```python
import functools
import math

import numpy as np
import jax
import jax.numpy as jnp
from jax import lax
from jax.experimental import pallas as pl
from jax.experimental.pallas import tpu as pltpu

F32 = jnp.float32
BF16 = jnp.bfloat16

N_MIXERS = 2
HY_ORDER = 2
HY_DIRS = 2
SHORT_CONV = 3
POS_EMB_DIM = 33
POS_BANDS = (POS_EMB_DIM - 1) // 2
DECAY_TARGET = 1e-2
FAST_DECAY_PCT = 0.3
SLOW_DECAY_PCT = 1.5
MIN_DECAY = math.log(DECAY_TARGET) / SLOW_DECAY_PCT
MAX_DECAY = math.log(DECAY_TARGET) / FAST_DECAY_PCT
N_KV_HEADS = 4
HEAD_DIM = 128
WINDOW = 128
LN_EPS = 1e-5
LOG2E = 1.4426950408889634
MASK_DIST = 1e30

FFT_N1 = 128
LANE = 128
BF16_ROWS = 16
F32_ROWS = 8
VMEM_LIMIT = 56 * 1024 * 1024


def _cparams(*sem, vmem_limit=VMEM_LIMIT):
    return pltpu.CompilerParams(dimension_semantics=sem, vmem_limit_bytes=vmem_limit)


def _swap_major_sublane(x):
    return jnp.swapaxes(x, 0, 1)


def _silu(x):
    return x * (0.5 * jnp.tanh(0.5 * x) + 0.5)


def _block(m):
    return np.block([[m.real, -m.imag], [m.imag, m.real]])


@functools.lru_cache(maxsize=None)
def _dft_tables(n):
    n1, n2 = FFT_N1, n // FFT_N1
    h1 = n1 // 2
    k1 = np.arange(n1)[:, None]
    ta_data, ta_filt = [], []
    for j in range(n2):
        m = np.exp(-2j * np.pi * (j * k1 / n + np.arange(n1)[None, :] * k1 / n1))
        ta_data.append(_block(m[:, :h1]))
        ta_filt.append(np.concatenate([m.real, m.imag], 0))
    f2 = np.exp(-2j * np.pi * np.arange(n2)[:, None] * np.arange(n2)[None, :] / n2)
    tb_fwd = _block(f2)
    t1 = np.arange(n2)[:, None]
    tb_inv = [_block(np.exp(2j * np.pi * (kk * t1 / n + np.arange(n2)[None, :] * t1 / n2)))
              for kk in range(n1)]
    hc = np.exp(2j * np.pi * np.arange(h1)[:, None] * np.arange(n1)[None, :] / n1) / n
    tc = _block(hc)
    f = lambda a: np.asarray(a, np.float32)
    pair = lambda t: np.concatenate([t[0::2], t[1::2]], axis=2)
    ta_data, tb_inv = np.stack(ta_data), np.stack(tb_inv)
    return dict(ta_filt=f(np.stack(ta_filt)), tb_fwd=f(tb_fwd), tc=f(tc),
                ta_pair=f(pair(ta_data)), tb_inv_pair=f(pair(tb_inv)),
                tb_fwd_pair=f(np.concatenate([tb_fwd, tb_fwd], axis=1)))


INPROJ_CHUNK = 1024


def _inproj_kernel(x_ref, w_ref, b_ref, cs_ref, o_ref, xb_ref, *, scaled):
    @pl.when(pl.program_id(1) == 0)
    def _():
        xb_ref[...] = x_ref[...].astype(BF16)

    tn = w_ref.shape[1]
    cw = min(tn, INPROJ_CHUNK)
    for c in range(tn // cw):
        sl = slice(c * cw, (c + 1) * cw)
        acc = jnp.dot(xb_ref[...], w_ref[:, sl], preferred_element_type=F32)
        if scaled:
            acc = acc * cs_ref[:, sl]
        val = (acc + b_ref[:, sl]).astype(o_ref.dtype)
        if len(o_ref.shape) == 3:
            n2, n1b, _ = o_ref.shape
            o_ref[:, :, sl] = _swap_major_sublane(val.reshape(n1b, n2, cw))
        else:
            o_ref[:, sl] = val


def _inproj(x, w, b, n_comp, col_scale=None, permute=None, tm=1024, tn=1024):
    m, k = x.shape
    n = w.shape[1]
    wc = n // n_comp
    tn = min(tn, wc)
    ncb = wc // tn
    scaled = col_scale is not None
    if not scaled:
        col_scale = b
    if permute is None:
        out_spec = pl.BlockSpec((None, tm, tn), lambda i, j: (j // ncb, i, j % ncb))
        out_shape = jax.ShapeDtypeStruct((n_comp, m, wc), BF16)
    else:
        l, n2 = permute
        tm = BF16_ROWS * n2
        tpb = l // tm
        out_spec = pl.BlockSpec((None, None, n2, BF16_ROWS, tn),
                                lambda i, j: (j // ncb, i // tpb, 0, i % tpb, j % ncb))
        out_shape = jax.ShapeDtypeStruct((n_comp, m // l, n2, l // n2, wc), BF16)
    vec = pl.BlockSpec((1, tn), lambda i, j: (0, j))
    return pl.pallas_call(
        functools.partial(_inproj_kernel, scaled=scaled),
        grid=(m // tm, n // tn),
        in_specs=[
            pl.BlockSpec((tm, k), lambda i, j: (i, 0)),
            pl.BlockSpec((k, tn), lambda i, j: (0, j)),
            vec, vec,
        ],
        out_specs=out_spec,
        out_shape=out_shape,
        scratch_shapes=[pltpu.VMEM((tm, k), BF16)],
        compiler_params=_cparams("parallel", "arbitrary"),
        name="inproj",
    )(x, w, b, col_scale)


def _outproj_ln_kernel(y_ref, w_ref, b_ref, x_ref, g_ref, beta_ref, o_ref, *, alpha, permuted):
    y = y_ref[...]
    if permuted:
        nb, na, kk = y.shape
        y = y.reshape(nb * na, kk)
    h = jnp.dot(y, w_ref[...], preferred_element_type=F32) + b_ref[...]
    if permuted:
        h = _swap_major_sublane(h.reshape(nb, na, h.shape[-1]))
    r = alpha * x_ref[...] + h
    mu = jnp.mean(r, axis=-1, keepdims=True)
    d = r - mu
    var = jnp.mean(d * d, axis=-1, keepdims=True)
    o_ref[...] = d * lax.rsqrt(var + LN_EPS) * g_ref[...] + beta_ref[...]


def _outproj_ln(y, w, b, x, g, beta, alpha, tm=512):
    m, k = y.shape
    d = w.shape[1]
    row = lambda i: (i, 0)
    fixed = lambda i: (0, 0)
    return pl.pallas_call(
        functools.partial(_outproj_ln_kernel, alpha=alpha, permuted=False),
        grid=(m // tm,),
        in_specs=[
            pl.BlockSpec((tm, k), row),
            pl.BlockSpec((k, d), fixed),
            pl.BlockSpec((1, d), fixed),
            pl.BlockSpec((tm, d), row),
            pl.BlockSpec((1, d), fixed),
            pl.BlockSpec((1, d), fixed),
        ],
        out_specs=pl.BlockSpec((tm, d), row),
        out_shape=jax.ShapeDtypeStruct((m, d), F32),
        compiler_params=_cparams("parallel"),
        name="outproj_ln",
    )(y, w, b, x, g, beta)


def _outproj_ln_permuted(y, w, b, x, g, beta, alpha):
    bsz, n2, h1, k = y.shape
    d = w.shape[1]
    nb = F32_ROWS
    fixed = lambda bi, j: (0, 0)
    xspec = pl.BlockSpec((None, h1, nb, d), lambda bi, j: (bi, 0, j, 0))
    return pl.pallas_call(
        functools.partial(_outproj_ln_kernel, alpha=alpha, permuted=True),
        grid=(bsz, n2 // nb),
        in_specs=[
            pl.BlockSpec((None, nb, h1, k), lambda bi, j: (bi, j, 0, 0)),
            pl.BlockSpec((k, d), fixed),
            pl.BlockSpec((1, d), fixed),
            xspec,
            pl.BlockSpec((1, d), fixed),
            pl.BlockSpec((1, d), fixed),
        ],
        out_specs=xspec,
        out_shape=jax.ShapeDtypeStruct((bsz, h1, n2, d), F32),
        compiler_params=_cparams("parallel", "parallel"),
        name="outproj_ln_perm",
    )(y, w, b, x, g, beta)


def _filter_kernel(feat_ref, w1_ref, b1_ref, fr1_ref, w2_ref, b2_ref, fr2_ref, w3_ref, b3_ref,
                   fr3_ref, w4_ref, delta_ref, k_ref, ss_ref, h_ref):
    half = pl.program_id(0)
    hp = lax.Precision.HIGHEST

    @pl.when((pl.program_id(1) == 0) & (pl.program_id(2) == 0))
    def _():
        h = jnp.sin(fr1_ref[...] * (jnp.dot(feat_ref[...], w1_ref[...], precision=hp,
                                             preferred_element_type=F32) + b1_ref[...]))
        h = jnp.sin(fr2_ref[...] * (jnp.dot(h, w2_ref[...], precision=hp,
                                             preferred_element_type=F32) + b2_ref[...]))
        h = jnp.sin(fr3_ref[...] * (jnp.dot(h, w3_ref[...], precision=hp,
                                             preferred_element_type=F32) + b3_ref[...]))
        h_hi = h.astype(BF16)
        h_lo = (h - h_hi.astype(F32)).astype(BF16)
        h_ref[...] = jnp.concatenate([h_hi, h_hi, h_lo], axis=1)

    w4 = w4_ref[...]
    w_hi = w4.astype(BF16)
    w_lo = (w4 - w_hi.astype(F32)).astype(BF16)
    k = jnp.dot(h_ref[...], jnp.concatenate([w_hi, w_lo, w_hi], axis=0), preferred_element_type=F32)
    t_norm = feat_ref[:, 0:1]
    k = k * jnp.exp(-t_norm * delta_ref[...])
    row = lax.broadcasted_iota(jnp.int32, k.shape, 0)
    k = jnp.where((row == 0) & (half == 1), 0.0, k)
    ss_ref[...] = jnp.sum(k * k, axis=0, keepdims=True)
    n2, rows, ec = k_ref.shape
    k_ref[...] = _swap_major_sublane(k.astype(k_ref.dtype).reshape(rows, n2, ec))


def _filters(l, e, n2, w_f1, b_f1, fr1, w_f2, b_f2, fr2, w_f3, b_f3, fr3, w_f4, ec=512):
    fh = w_f1.shape[1]
    ec = min(ec, e)
    t_norm = jnp.linspace(0.0, 1.0, l, dtype=F32)
    w = 2.0 * math.pi * jnp.arange(l, dtype=F32) / l
    f = jnp.linspace(1e-4, POS_BANDS - 1, POS_BANDS, dtype=F32)
    ang = w[:, None] * f[None, :]
    feat = jnp.concatenate([t_norm[:, None], jnp.cos(ang), -jnp.sin(ang)], axis=-1)
    feat_rev = jnp.concatenate([feat[:1], feat[:0:-1]], axis=0)
    feat2 = jnp.pad(jnp.concatenate([feat, feat_rev], axis=0), ((0, 0), (0, LANE - POS_EMB_DIM)))
    w1p = jnp.pad(w_f1, ((0, LANE - POS_EMB_DIM), (0, 0)))
    deltas = jnp.abs(jnp.linspace(MIN_DECAY, MAX_DECAY, e, dtype=F32))[None, :]
    nec = e // ec
    rows = l // n2
    vec = lambda a: a.reshape(1, -1)
    fixed = lambda h, o, j: (0, 0)
    return pl.pallas_call(
        _filter_kernel,
        grid=(HY_DIRS, HY_ORDER, nec),
        in_specs=[
            pl.BlockSpec((l, LANE), lambda h, o, j: (h, 0)),
            pl.BlockSpec((LANE, fh), fixed), pl.BlockSpec((1, fh), fixed), pl.BlockSpec((1, fh), fixed),
            pl.BlockSpec((fh, fh), fixed), pl.BlockSpec((1, fh), fixed), pl.BlockSpec((1, fh), fixed),
            pl.BlockSpec((fh, fh), fixed), pl.BlockSpec((1, fh), fixed), pl.BlockSpec((1, fh), fixed),
            pl.BlockSpec((fh, ec), lambda h, o, j: (0, (o * HY_DIRS + h) * nec + j)),
            pl.BlockSpec((1, ec), lambda h, o, j: (0, j)),
        ],
        out_specs=[
            pl.BlockSpec((None, n2, rows, ec), lambda h, o, j: (o, 0, h, j)),
            pl.BlockSpec((None, None, 1, ec), lambda h, o, j: (o, h, 0, j)),
        ],
        out_shape=[
            jax.ShapeDtypeStruct((HY_ORDER, n2, HY_DIRS * rows, e), BF16),
            jax.ShapeDtypeStruct((HY_ORDER, HY_DIRS, 1, e), F32),
        ],
        scratch_shapes=[pltpu.VMEM((l, 3 * fh), BF16)],
        compiler_params=_cparams("arbitrary", "arbitrary", "arbitrary"),
        name="hyena_filter",
    )(feat2, w1p, vec(b_f1), vec(fr1), w_f2, vec(b_f2), vec(fr2), w_f3, vec(b_f3), vec(fr3),
      w_f4, deltas)


FFT_G = BF16_ROWS
FFT_EC = 512


def _fft_a_kernel(z_ref, ta_ref, are_ref, aim_ref):
    g = ta_ref.shape[0]
    n1 = are_ref.shape[0]
    rs = []
    for j in range(g):
        z = z_ref[:, j].reshape(ta_ref.shape[2], z_ref.shape[-1])
        rs.append(jnp.dot(ta_ref[j], z, preferred_element_type=F32).astype(BF16))
    rt = _swap_major_sublane(jnp.stack(rs))
    are_ref[...] = rt[:n1]
    aim_ref[...] = rt[n1:]


def _fft_a(z6, comp, ta, ec=FFT_EC, g=FFT_G):
    _, p, s, n2, r, e = z6.shape
    _, m2, kk = ta.shape
    n1 = m2 // 2
    ec = min(ec, e)
    out = jax.ShapeDtypeStruct((p, n1, n2, e), BF16)
    ospec = pl.BlockSpec((None, n1, g, ec), lambda pi, j, c: (pi, 0, j, c))
    return pl.pallas_call(
        _fft_a_kernel,
        grid=(p, n2 // g, e // ec),
        in_specs=[
            pl.BlockSpec((None, None, s, g, r, ec), lambda pi, j, c: (comp, pi, 0, j, 0, c)),
            pl.BlockSpec((g, m2, kk), lambda pi, j, c: (j, 0, 0)),
        ],
        out_specs=[ospec, ospec],
        out_shape=[out, out],
        compiler_params=_cparams("parallel", "arbitrary", "arbitrary"),
        name="fft_a",
    )(z6, ta)


def _fft_bf_kernel(are_ref, aim_ref, tf_ref, ss_ref, kre_ref, kim_ref):
    g, n2, _ = are_ref.shape
    rs = lax.rsqrt(ss_ref[0] + ss_ref[1] + 1e-12)
    for j in range(g):
        a = jnp.concatenate([are_ref[j], aim_ref[j]], axis=0)
        x = jnp.dot(tf_ref[...], a, preferred_element_type=F32) * rs
        kre_ref[j] = x[:n2].astype(kre_ref.dtype)
        kim_ref[j] = x[n2:].astype(kim_ref.dtype)


def _fft_b_filter(are, aim, tf, ss, ec=FFT_EC, g=FFT_G):
    p, n1, n2, e = are.shape
    ec = min(ec, e)
    spec = pl.BlockSpec((None, g, n2, ec), lambda j, c, pi: (pi, j, 0, c))
    out = jax.ShapeDtypeStruct((p, n1, n2, e), BF16)
    return pl.pallas_call(
        _fft_bf_kernel,
        grid=(n1 // g, e // ec, p),
        in_specs=[spec, spec,
                  pl.BlockSpec((2 * n2, 2 * n2), lambda j, c, pi: (0, 0)),
                  pl.BlockSpec((None, HY_DIRS, 1, ec), lambda j, c, pi: (pi, 0, 0, c))],
        out_specs=[spec, spec],
        out_shape=[out, out],
        compiler_params=_cparams("parallel", "arbitrary", "arbitrary"),
        name="fft_b_filter",
    )(are, aim, tf, ss)


HC_EC = LANE
HC_G = BF16_ROWS
STAGE_B_SPLIT = 1
HC_VMEM_LIMIT = 60 * 1024 * 1024


def _blockdiag(a, b):
    z = jnp.zeros_like(a)
    return jnp.concatenate([jnp.concatenate([a, z], axis=1),
                            jnp.concatenate([z, b], axis=1)], axis=0)


def _hyena_core_kernel(p_ref, wsc_ref, bsc_ref, hb_ref, kre_ref, kim_ref, tap_ref, tfp_ref,
                       tgp_ref, tc_ref, y_ref, a_scr, c_scr, z_scr):
    _, two, n2, h1, ec = p_ref.shape
    n1, g = a_scr.shape[2], a_scr.shape[3]
    rows = two * h1
    ngrp_a, ngrp_b = n2 // g, n1 // g
    half = g // 2
    rowi = lax.broadcasted_iota(jnp.int32, (rows, ec), 0) & (h1 - 1)

    def slab(comp, idx):
        return p_ref[comp, :, idx].reshape(rows, ec).astype(F32)

    def conv_group(comp, gi):
        base = gi * g
        first, last = gi == 0, gi == ngrp_a - 1
        w = [wsc_ref[t, comp:comp + 1, :] for t in range(SHORT_CONV)]
        b = bsc_ref[comp:comp + 1, :]
        prev = slab(comp, jnp.where(first, n2 - 1, base - 1))
        prev = jnp.where(first, jnp.where(rowi == 0, 0.0, pltpu.roll(prev, 1, axis=0)), prev)
        cur = slab(comp, base)
        for j in range(g):
            if j < g - 1:
                nxt = slab(comp, base + j + 1)
            else:
                nxt = slab(comp, jnp.where(last, 0, base + g))
                nxt = jnp.where(last, jnp.where(rowi == h1 - 1, 0.0,
                                                pltpu.roll(nxt, rows - 1, axis=0)), nxt)
            yield b + w[0] * prev + w[1] * cur + w[2] * nxt
            prev, cur = cur, nxt

    def stage_a_group(gi, zs):
        outs = []
        for jj in range(half):
            r = jnp.dot(tap_ref[gi * half + jj], _blockdiag(zs[2 * jj], zs[2 * jj + 1]),
                        preferred_element_type=F32).astype(BF16)
            outs += [r[:, :ec], r[:, ec:]]
        rt = _swap_major_sublane(jnp.stack(outs))
        a_scr[0, gi] = rt[:n1]
        a_scr[1, gi] = rt[n1:]

    def stage_b(order):
        def col(k1):
            return jnp.concatenate([a_scr[0, :, k1].reshape(n2, ec),
                                    a_scr[1, :, k1].reshape(n2, ec)], axis=0)

        def body(kb, carry):
            cs = []
            sub = half // STAGE_B_SPLIT
            for s0 in range(0, half, sub):
                pairs = range(s0, s0 + sub)
                xs = [jnp.dot(tfp_ref[...], _blockdiag(col(kb * g + 2 * jj), col(kb * g + 2 * jj + 1)),
                              preferred_element_type=F32) for jj in pairs]
                rhs = []
                for x, jj in zip(xs, pairs):
                    ka = kb * g + 2 * jj
                    kr = jnp.concatenate([kre_ref[order, ka], kre_ref[order, ka + 1]], axis=1).astype(F32)
                    ki = jnp.concatenate([kim_ref[order, ka], kim_ref[order, ka + 1]], axis=1).astype(F32)
                    xr, xi = x[:n2], x[n2:]
                    yr = (xr * kr - xi * ki).astype(BF16)
                    yi = (xr * ki + xi * kr).astype(BF16)
                    rhs.append(_blockdiag(jnp.concatenate([yr[:, :ec], yi[:, :ec]], axis=0),
                                          jnp.concatenate([yr[:, ec:], yi[:, ec:]], axis=0)))
                for r, jj in zip(rhs, pairs):
                    c = jnp.dot(tgp_ref[kb * half + jj], r,
                                preferred_element_type=F32).astype(BF16)
                    cs += [c[:, :ec], c[:, ec:]]
            ct = _swap_major_sublane(jnp.stack(cs))
            c_scr[0, kb] = ct[:n2]
            c_scr[1, kb] = ct[n2:]
            return carry

        lax.fori_loop(0, ngrp_b, body, 0, unroll=True)

    def stage_c_slab(t1):
        c = jnp.concatenate([c_scr[0, :, t1].reshape(n1, ec), c_scr[1, :, t1].reshape(n1, ec)], axis=0)
        return jnp.dot(tc_ref[...], c, preferred_element_type=F32)

    def body_a1(gi, carry):
        zs = []
        for j, v in enumerate(conv_group(0, gi)):
            zs.append(v.astype(BF16))
            z_scr[gi * g + j] = zs[j]
        stage_a_group(gi, zs)
        return carry

    def body_c1(gi, carry):
        zs = []
        for j, gate in enumerate(conv_group(1, gi)):
            t1 = gi * g + j
            z1 = gate * (stage_c_slab(t1) + hb_ref[0:1, :] * z_scr[t1].astype(F32))
            zs.append(z1.astype(BF16))
            z_scr[t1] = zs[j]
        stage_a_group(gi, zs)
        return carry

    def body_c2(gi, carry):
        for j, gate in enumerate(conv_group(2, gi)):
            t1 = gi * g + j
            z2 = gate * (stage_c_slab(t1) + hb_ref[1:2, :] * z_scr[t1].astype(F32))
            out = z2 * _silu(slab(3, t1))
            y_ref[:, t1] = out.astype(BF16).reshape(two, h1, ec)
        return carry

    lax.fori_loop(0, ngrp_a, body_a1, 0, unroll=True)
    stage_b(0)
    lax.fori_loop(0, ngrp_a, body_c1, 0, unroll=True)
    stage_b(1)
    lax.fori_loop(0, ngrp_a, body_c2, 0, unroll=True)


def _hyena_core(p6, w_sc, b_sc, h_bias, kre, kim, tap, tfp, tgp, tc, ec=HC_EC, g=HC_G):
    _, p, two, n2, h1, e = p6.shape
    n1 = kre.shape[1]
    ec = min(ec, e)
    once = pl.Buffered(1)
    const = lambda shape: pl.BlockSpec(shape, lambda c, pi: (0,) * len(shape), pipeline_mode=once)
    kspec = pl.BlockSpec((HY_ORDER, n1, n2, ec), lambda c, pi: (0, 0, 0, c))
    return pl.pallas_call(
        _hyena_core_kernel,
        grid=(e // ec, p),
        in_specs=[
            pl.BlockSpec((4, None, two, n2, h1, ec), lambda c, pi: (0, pi, 0, 0, 0, c)),
            pl.BlockSpec((SHORT_CONV, 3, ec), lambda c, pi: (0, 0, c)),
            pl.BlockSpec((3, ec), lambda c, pi: (0, c)),
            pl.BlockSpec((HY_ORDER, ec), lambda c, pi: (0, c)),
            kspec, kspec,
            const(tap.shape), const(tfp.shape), const(tgp.shape), const(tc.shape),
        ],
        out_specs=pl.BlockSpec((None, two, n2, h1, ec), lambda c, pi: (pi, 0, 0, 0, c)),
        out_shape=jax.ShapeDtypeStruct((p, two, n2, h1, e), BF16),
        scratch_shapes=[pltpu.VMEM((2, n2 // g, n1, g, ec), BF16),
                        pltpu.VMEM((2, n1 // g, n2, g, ec), BF16),
                        pltpu.VMEM((n2, two * h1, ec), BF16)],
        compiler_params=_cparams("parallel", "arbitrary", vmem_limit=HC_VMEM_LIMIT),
        name="hyena_core",
    )(p6, w_sc.reshape(SHORT_CONV, 3, e), b_sc.reshape(3, e), h_bias, kre, kim, tap, tfp, tgp, tc)


ATT_QB = 8


def _attn_kernel(slope_ref, sink_ref, q_ref, *refs, group):
    nkb = ATT_QB + 2
    k_refs, v_refs = refs[:nkb], refs[nkb:2 * nkb]
    glo_ref, ghi_ref, o_ref = refs[2 * nkb:]
    for s in range(ATT_QB):
        rows = slice(s * WINDOW, (s + 1) * WINDOW)
        _attn_block(slope_ref, sink_ref, q_ref.at[rows], k_refs[s:s + 3], v_refs[s:s + 3],
                    glo_ref.at[rows], ghi_ref.at[rows], o_ref.at[rows],
                    pl.program_id(1) * ATT_QB + s, pl.num_programs(1) * ATT_QB, group)


def _attn_block(slope_ref, sink_ref, q_ref, k_refs, v_refs, glo_ref, ghi_ref, o_ref, n, nb, group):
    kp_ref, kc_ref, kn_ref = k_refs
    vp_ref, vc_ref, vn_ref = v_refs
    blk = q_ref.shape[0]
    hd = HEAD_DIM
    n_kv = kc_ref.shape[1] // hd
    heads_per_g = glo_ref.shape[1] // hd
    nk = 3 * blk
    sj = lax.broadcasted_iota(jnp.int32, (nk, blk), 0)
    qi = lax.broadcasted_iota(jnp.int32, (nk, blk), 1)
    dist = jnp.abs(qi - sj + blk)
    valid = (dist <= WINDOW) & ((sj >= blk) | (n > 0)) & ((sj < 2 * blk) | (n < nb - 1))
    distm = jnp.where(valid, dist.astype(F32), MASK_DIST)
    pad = BF16_ROWS
    first_q = lax.broadcasted_iota(jnp.int32, (pad, blk), 0) == 0
    first_d = lax.broadcasted_iota(jnp.int32, (pad, hd), 0) == 0
    ones_blk = jnp.ones((nk, hd), BF16)
    sink_v = jnp.concatenate([jnp.zeros((pad, hd), BF16),
                              jnp.where(first_d, 1.0, 0.0).astype(BF16)], axis=1)
    for kvh in range(n_kv):
        ksl = slice(kvh * hd, (kvh + 1) * hd)
        k = jnp.concatenate([kp_ref[:, ksl], kc_ref[:, ksl], kn_ref[:, ksl]], axis=0)
        v = jnp.concatenate([vp_ref[:, ksl], vc_ref[:, ksl], vn_ref[:, ksl]], axis=0)
        v_aug = jnp.concatenate([jnp.concatenate([v, ones_blk], axis=1), sink_v], axis=0)
        heads = range(kvh * group, (kvh + 1) * group)
        q = jnp.concatenate([q_ref[:, h * hd:(h + 1) * hd] for h in heads], axis=0)
        st = lax.dot_general(k, q, (((1,), (1,)), ((), ())), preferred_element_type=F32)
        pts = []
        for gi, h in enumerate(heads):
            logit = st[:, gi * blk:(gi + 1) * blk] - slope_ref[h] * distm
            sk = sink_ref[h]
            m = jnp.maximum(jnp.max(logit, axis=0, keepdims=True), sk)
            p = jnp.exp2(logit - m)
            sink_p = jnp.where(first_q, jnp.exp2(sk - m), 0.0)
            pts.append(jnp.concatenate([p, sink_p], axis=0).astype(BF16))
        oa = lax.dot_general(jnp.concatenate(pts, axis=1), v_aug, (((0,), (0,)), ((), ())),
                             preferred_element_type=F32)
        for gi, h in enumerate(heads):
            rows = slice(gi * blk, (gi + 1) * blk)
            o = oa[rows, :hd] * (1.0 / oa[rows, hd:])
            g_half = glo_ref if h < heads_per_g else ghi_ref
            hg = h % heads_per_g
            gate = _silu(g_half[:, hg * hd:(hg + 1) * hd].astype(F32))
            o_ref[:, h * hd:(h + 1) * hd] = (o * gate).astype(o_ref.dtype)


def _attention(proj, slopes, sink, n_heads):
    bsz, l, _ = proj.shape
    hd, blk = HEAD_DIM, WINDOW
    group = n_heads // N_KV_HEADS
    aw = n_heads * hd
    kvw = N_KV_HEADS * hd
    nb = l // blk
    k0 = aw // kvw
    v0 = k0 + 1
    gw = aw // 2
    g0 = (aw + 2 * kvw) // gw
    smem = pl.BlockSpec(memory_space=pltpu.SMEM)
    qb = ATT_QB
    kv = lambda c0, off: pl.BlockSpec(
        (None, blk, kvw), lambda b, n: (b, jnp.clip(qb * n + off, 0, nb - 1), c0))
    kvs = lambda c0: [kv(c0, off) for off in range(-1, qb + 1)]
    gspec = lambda c0: pl.BlockSpec((None, qb * blk, gw), lambda b, n: (b, n, c0))
    assert (aw + 2 * kvw) % gw == 0 and aw % kvw == 0 and nb % qb == 0
    n_in = 3 + 2 * (qb + 2) + 2
    return pl.pallas_call(
        functools.partial(_attn_kernel, group=group),
        grid=(bsz, nb // qb),
        in_specs=[smem, smem,
                  pl.BlockSpec((None, qb * blk, aw), lambda b, n: (b, n, 0)),
                  *kvs(k0), *kvs(v0), gspec(g0), gspec(g0 + 1)],
        out_specs=pl.BlockSpec((None, qb * blk, aw), lambda b, n: (b, n, 0)),
        out_shape=jax.ShapeDtypeStruct((bsz, l, aw), BF16),
        compiler_params=_cparams("parallel", "arbitrary"),
        name="swa_attention",
    )(slopes, sink, *([proj] * (n_in - 2)))


def _hyena_filter_spectrum(l, e, hy, tabs):
    n1 = FFT_N1
    n2 = 2 * l // n1
    k_un, ss = _filters(l, e, n2, hy["w_f1"], hy["b_f1"], hy["fr1"], hy["w_f2"], hy["b_f2"],
                        hy["fr2"], hy["w_f3"], hy["b_f3"], hy["fr3"], hy["w_f4"])
    k6 = k_un.reshape(1, HY_ORDER, 1, n2, n1, e)
    are, aim = _fft_a(k6, 0, tabs["ta_filt"], ec=2 * FFT_EC)
    return _fft_b_filter(are, aim, tabs["tb_fwd"], ss, ec=4 * FFT_EC)


def _hyena_layer(x, hy, kf, tabs, ln_g, ln_b, alpha):
    bsz, l, d = x.shape
    e = hy["w_out"].shape[0]
    kre, kim = kf
    n1 = FFT_N1
    n2 = 2 * l // n1
    h1 = n1 // 2
    proj = _inproj(x.reshape(bsz * l, d), hy["w_in"], hy["b_in"], 4, permute=(l, n2),
                   tn=2 * INPROJ_CHUNK)
    y = _hyena_core(proj.reshape(4, bsz // 2, 2, n2, h1, e), hy["w_sc"], hy["b_sc"], hy["h_bias"],
                    kre, kim, tabs["ta_pair"], tabs["tb_fwd_pair"], tabs["tb_inv_pair"],
                    tabs["tc"])
    out = _outproj_ln_permuted(y.reshape(bsz, n2, h1, e), hy["w_out"], hy["b_out"],
                               x.reshape(bsz, h1, n2, d), ln_g, ln_b, alpha)
    return out.reshape(bsz, l, d)


def _attention_layer(x, at, ln_g, ln_b, alpha):
    bsz, l, d = x.shape
    n_heads = at["sink"].shape[0]
    x2 = x.reshape(bsz * l, d)
    proj = _inproj(x2, at["w_in"], at["b_in"], 1, col_scale=at["col_scale"])
    proj = proj.reshape(bsz, l, -1)
    o = _attention(proj, at["slopes"], at["sink"] * LOG2E, n_heads)
    out = _outproj_ln(o.reshape(bsz * l, -1), at["w_out"], at["b_out"], x2, ln_g, ln_b, alpha)
    return out.reshape(bsz, l, d)


def kernel(x_prompt, x_sample, ln_g, ln_b, hy_w_in, hy_b_in, hy_w_sc, hy_b_sc, hy_w_f1, hy_b_f1,
           hy_fr1, hy_w_f2, hy_b_f2, hy_fr2, hy_w_f3, hy_b_f3, hy_fr3, hy_w_f4, hy_h_bias,
           hy_w_out, hy_b_out, at_w_in, at_sink, at_w_out):
    depth = ln_g.shape[0]
    alpha = (2 * depth) ** 0.25
    l = x_prompt.shape[1]
    d = x_prompt.shape[2]
    assert x_sample.shape[1] == l and (2 * l) % FFT_N1 == 0
    assert x_prompt.shape[0] % 2 == 0 and x_sample.shape[0] % 2 == 0
    tabs = {k: jnp.asarray(t).astype(BF16) for k, t in _dft_tables(2 * l).items()}
    row = lambda a: a.reshape(1, -1)

    layers = []
    for i in range(depth):
        j = i // N_MIXERS
        if i % N_MIXERS == 0:
            e = hy_w_out.shape[1]
            hy = dict(w_in=hy_w_in[j].astype(BF16), b_in=row(hy_b_in[j]), w_sc=hy_w_sc[j],
                      b_sc=row(hy_b_sc[j]), w_f1=hy_w_f1[j], b_f1=hy_b_f1[j], fr1=hy_fr1[j],
                      w_f2=hy_w_f2[j], b_f2=hy_b_f2[j], fr2=hy_fr2[j], w_f3=hy_w_f3[j],
                      b_f3=hy_b_f3[j], fr3=hy_fr3[j], w_f4=hy_w_f4[j], h_bias=hy_h_bias[j],
                      w_out=hy_w_out[j].astype(BF16), b_out=row(hy_b_out[j]))
            kf = _hyena_filter_spectrum(l, e, hy, tabs)
            layers.append(("hyena", hy, kf))
        else:
            n_heads = at_sink.shape[1]
            aw = n_heads * HEAD_DIM
            width = at_w_in.shape[2]
            col_scale = jnp.where(jnp.arange(width) < aw, HEAD_DIM ** -0.5 * LOG2E, 1.0)
            slopes = jnp.exp2(-8.0 * jnp.arange(1, n_heads + 1, dtype=F32) / n_heads) * LOG2E
            at = dict(w_in=at_w_in[j].astype(BF16), b_in=jnp.zeros((1, width), F32),
                      col_scale=row(col_scale.astype(F32)), slopes=slopes,
                      sink=at_sink[j], w_out=at_w_out[j].astype(BF16),
                      b_out=jnp.zeros((1, d), F32))
            layers.append(("attn", at, None))

    def trunk(x):
        for i, (kind, prm, kf) in enumerate(layers):
            g, b = row(ln_g[i]), row(ln_b[i])
            if kind == "hyena":
                x = _hyena_layer(x, prm, kf, tabs, g, b, alpha)
            else:
                x = _attention_layer(x, prm, g, b, alpha)
        return x

    return (trunk(x_prompt), trunk(x_sample))
```

```python
import functools
import math

import numpy as np
import jax
import jax.numpy as jnp
from jax import lax
from jax.experimental import pallas as pl
from jax.experimental.pallas import tpu as pltpu

F32 = jnp.float32
BF16 = jnp.bfloat16

N_MIXERS = 2
HY_ORDER = 2
HY_DIRS = 2
SHORT_CONV = 3
POS_EMB_DIM = 33
POS_BANDS = (POS_EMB_DIM - 1) // 2
DECAY_TARGET = 1e-2
FAST_DECAY_PCT = 0.3
SLOW_DECAY_PCT = 1.5
MIN_DECAY = math.log(DECAY_TARGET) / SLOW_DECAY_PCT
MAX_DECAY = math.log(DECAY_TARGET) / FAST_DECAY_PCT
N_KV_HEADS = 4
HEAD_DIM = 128
WINDOW = 128
LN_EPS = 1e-5
LOG2E = 1.4426950408889634
MASK_DIST = 1e30

FFT_N1 = 128
LANE = 128
BF16_ROWS = 16
F32_ROWS = 8
VMEM_LIMIT = 56 * 1024 * 1024


def _cparams(*sem, vmem_limit=VMEM_LIMIT):
    return pltpu.CompilerParams(dimension_semantics=sem, vmem_limit_bytes=vmem_limit)


def _swap_major_sublane(x):
    return jnp.swapaxes(x, 0, 1)


def _silu(x):
    return x * (0.5 * jnp.tanh(0.5 * x) + 0.5)


def _block(m):
    return np.block([[m.real, -m.imag], [m.imag, m.real]])


@functools.lru_cache(maxsize=None)
def _dft_tables(n):
    n1, n2 = FFT_N1, n // FFT_N1
    h1 = n1 // 2
    k1 = np.arange(n1)[:, None]
    ta_data, ta_filt = [], []
    for j in range(n2):
        m = np.exp(-2j * np.pi * (j * k1 / n + np.arange(n1)[None, :] * k1 / n1))
        ta_data.append(_block(m[:, :h1]))
        ta_filt.append(np.concatenate([m.real, m.imag], 0))
    f2 = np.exp(-2j * np.pi * np.arange(n2)[:, None] * np.arange(n2)[None, :] / n2)
    tb_fwd = _block(f2)
    t1 = np.arange(n2)[:, None]
    tb_inv = [_block(np.exp(2j * np.pi * (kk * t1 / n + np.arange(n2)[None, :] * t1 / n2)))
              for kk in range(n1)]
    hc = np.exp(2j * np.pi * np.arange(h1)[:, None] * np.arange(n1)[None, :] / n1) / n
    tc = _block(hc)
    f = lambda a: np.asarray(a, np.float32)
    pair = lambda t: np.concatenate([t[0::2], t[1::2]], axis=2)
    ta_data, tb_inv = np.stack(ta_data), np.stack(tb_inv)
    return dict(ta_filt=f(np.stack(ta_filt)), tb_fwd=f(tb_fwd), tc=f(tc),
                ta_pair=f(pair(ta_data)), tb_inv_pair=f(pair(tb_inv)),
                tb_fwd_pair=f(np.concatenate([tb_fwd, tb_fwd], axis=1)))


INPROJ_CHUNK = 1024


def _inproj_kernel(x_ref, w_ref, b_ref, cs_ref, o_ref, xb_ref, *, scaled):
    @pl.when(pl.program_id(1) == 0)
    def _():
        xb_ref[...] = x_ref[...].astype(BF16)

    tn = w_ref.shape[1]
    cw = min(tn, INPROJ_CHUNK)
    for c in range(tn // cw):
        sl = slice(c * cw, (c + 1) * cw)
        acc = jnp.dot(xb_ref[...], w_ref[:, sl], preferred_element_type=F32)
        if scaled:
            acc = acc * cs_ref[:, sl]
        val = (acc + b_ref[:, sl]).astype(o_ref.dtype)
        if len(o_ref.shape) == 3:
            n2, n1b, _ = o_ref.shape
            o_ref[:, :, sl] = _swap_major_sublane(val.reshape(n1b, n2, cw))
        else:
            o_ref[:, sl] = val


def _inproj(x, w, b, n_comp, col_scale=None, permute=None, tm=1024, tn=1024):
    m, k = x.shape
    n = w.shape[1]
    wc = n // n_comp
    tn = min(tn, wc)
    ncb = wc // tn
    scaled = col_scale is not None
    if not scaled:
        col_scale = b
    if permute is None:
        out_spec = pl.BlockSpec((None, tm, tn), lambda i, j: (j // ncb, i, j % ncb))
        out_shape = jax.ShapeDtypeStruct((n_comp, m, wc), BF16)
    else:
        l, n2 = permute
        tm = BF16_ROWS * n2
        tpb = l // tm
        out_spec = pl.BlockSpec((None, None, n2, BF16_ROWS, tn),
                                lambda i, j: (j // ncb, i // tpb, 0, i % tpb, j % ncb))
        out_shape = jax.ShapeDtypeStruct((n_comp, m // l, n2, l // n2, wc), BF16)
    vec = pl.BlockSpec((1, tn), lambda i, j: (0, j))
    w_mode = {"pipeline_mode": pl.Buffered(1)} if tn == n else {}
    return pl.pallas_call(
        functools.partial(_inproj_kernel, scaled=scaled),
        grid=(m // tm, n // tn),
        in_specs=[
            pl.BlockSpec((tm, k), lambda i, j: (i, 0)),
            pl.BlockSpec((k, tn), lambda i, j: (0, j), **w_mode),
            vec, vec,
        ],
        out_specs=out_spec,
        out_shape=out_shape,
        scratch_shapes=[pltpu.VMEM((tm, k), BF16)],
        compiler_params=_cparams("parallel", "arbitrary"),
        name="inproj",
    )(x, w, b, col_scale)


def _outproj_ln_kernel(y_ref, w_ref, b_ref, x_ref, g_ref, beta_ref, o_ref, *, alpha, permuted):
    y = y_ref[...]
    if permuted:
        nb, na, kk = y.shape
        y = y.reshape(nb * na, kk)
    h = jnp.dot(y, w_ref[...], preferred_element_type=F32) + b_ref[...]
    if permuted:
        h = _swap_major_sublane(h.reshape(nb, na, h.shape[-1]))
    r = alpha * x_ref[...] + h
    mu = jnp.mean(r, axis=-1, keepdims=True)
    d = r - mu
    var = jnp.mean(d * d, axis=-1, keepdims=True)
    o_ref[...] = d * lax.rsqrt(var + LN_EPS) * g_ref[...] + beta_ref[...]


def _outproj_ln(y, w, b, x, g, beta, alpha, tm=512):
    m, k = y.shape
    d = w.shape[1]
    row = lambda i: (i, 0)
    fixed = lambda i: (0, 0)
    return pl.pallas_call(
        functools.partial(_outproj_ln_kernel, alpha=alpha, permuted=False),
        grid=(m // tm,),
        in_specs=[
            pl.BlockSpec((tm, k), row),
            pl.BlockSpec((k, d), fixed),
            pl.BlockSpec((1, d), fixed),
            pl.BlockSpec((tm, d), row),
            pl.BlockSpec((1, d), fixed),
            pl.BlockSpec((1, d), fixed),
        ],
        out_specs=pl.BlockSpec((tm, d), row),
        out_shape=jax.ShapeDtypeStruct((m, d), F32),
        compiler_params=_cparams("parallel"),
        name="outproj_ln",
    )(y, w, b, x, g, beta)


def _outproj_ln_permuted(y, w, b, x, g, beta, alpha):
    bsz, n2, h1, k = y.shape
    d = w.shape[1]
    nb = F32_ROWS
    fixed = lambda bi, j: (0, 0)
    xspec = pl.BlockSpec((None, h1, nb, d), lambda bi, j: (bi, 0, j, 0))
    return pl.pallas_call(
        functools.partial(_outproj_ln_kernel, alpha=alpha, permuted=True),
        grid=(bsz, n2 // nb),
        in_specs=[
            pl.BlockSpec((None, nb, h1, k), lambda bi, j: (bi, j, 0, 0)),
            pl.BlockSpec((k, d), fixed),
            pl.BlockSpec((1, d), fixed),
            xspec,
            pl.BlockSpec((1, d), fixed),
            pl.BlockSpec((1, d), fixed),
        ],
        out_specs=xspec,
        out_shape=jax.ShapeDtypeStruct((bsz, h1, n2, d), F32),
        compiler_params=_cparams("parallel", "parallel"),
        name="outproj_ln_perm",
    )(y, w, b, x, g, beta)


def _filter_kernel(feat_ref, w1_ref, b1_ref, fr1_ref, w2_ref, b2_ref, fr2_ref, w3_ref, b3_ref,
                   fr3_ref, w4_ref, delta_ref, k_ref, ss_ref, h_ref):
    half = pl.program_id(0)
    hp = lax.Precision.HIGHEST

    @pl.when((pl.program_id(1) == 0) & (pl.program_id(2) == 0))
    def _():
        h = jnp.sin(fr1_ref[...] * (jnp.dot(feat_ref[...], w1_ref[...], precision=hp,
                                             preferred_element_type=F32) + b1_ref[...]))
        h = jnp.sin(fr2_ref[...] * (jnp.dot(h, w2_ref[...], precision=hp,
                                             preferred_element_type=F32) + b2_ref[...]))
        h = jnp.sin(fr3_ref[...] * (jnp.dot(h, w3_ref[...], precision=hp,
                                             preferred_element_type=F32) + b3_ref[...]))
        h_hi = h.astype(BF16)
        h_lo = (h - h_hi.astype(F32)).astype(BF16)
        h_ref[...] = jnp.concatenate([h_hi, h_hi, h_lo], axis=1)

    w4 = w4_ref[...]
    w_hi = w4.astype(BF16)
    w_lo = (w4 - w_hi.astype(F32)).astype(BF16)
    k = jnp.dot(h_ref[...], jnp.concatenate([w_hi, w_lo, w_hi], axis=0), preferred_element_type=F32)
    t_norm = feat_ref[:, 0:1]
    k = k * jnp.exp(-t_norm * delta_ref[...])
    row = lax.broadcasted_iota(jnp.int32, k.shape, 0)
    k = jnp.where((row == 0) & (half == 1), 0.0, k)
    ss_ref[...] = jnp.sum(k * k, axis=0, keepdims=True)
    n2, rows, ec = k_ref.shape
    k_ref[...] = _swap_major_sublane(k.astype(k_ref.dtype).reshape(rows, n2, ec))


def _filters(l, e, n2, w_f1, b_f1, fr1, w_f2, b_f2, fr2, w_f3, b_f3, fr3, w_f4, ec=512):
    fh = w_f1.shape[1]
    ec = min(ec, e)
    t_norm = jnp.linspace(0.0, 1.0, l, dtype=F32)
    w = 2.0 * math.pi * jnp.arange(l, dtype=F32) / l
    f = jnp.linspace(1e-4, POS_BANDS - 1, POS_BANDS, dtype=F32)
    ang = w[:, None] * f[None, :]
    feat = jnp.concatenate([t_norm[:, None], jnp.cos(ang), -jnp.sin(ang)], axis=-1)
    feat_rev = jnp.concatenate([feat[:1], feat[:0:-1]], axis=0)
    feat2 = jnp.pad(jnp.concatenate([feat, feat_rev], axis=0), ((0, 0), (0, LANE - POS_EMB_DIM)))
    w1p = jnp.pad(w_f1, ((0, LANE - POS_EMB_DIM), (0, 0)))
    deltas = jnp.abs(jnp.linspace(MIN_DECAY, MAX_DECAY, e, dtype=F32))[None, :]
    nec = e // ec
    rows = l // n2
    vec = lambda a: a.reshape(1, -1)
    fixed = lambda h, o, j: (0, 0)
    return pl.pallas_call(
        _filter_kernel,
        grid=(HY_DIRS, HY_ORDER, nec),
        in_specs=[
            pl.BlockSpec((l, LANE), lambda h, o, j: (h, 0)),
            pl.BlockSpec((LANE, fh), fixed), pl.BlockSpec((1, fh), fixed), pl.BlockSpec((1, fh), fixed),
            pl.BlockSpec((fh, fh), fixed), pl.BlockSpec((1, fh), fixed), pl.BlockSpec((1, fh), fixed),
            pl.BlockSpec((fh, fh), fixed), pl.BlockSpec((1, fh), fixed), pl.BlockSpec((1, fh), fixed),
            pl.BlockSpec((fh, ec), lambda h, o, j: (0, (o * HY_DIRS + h) * nec + j)),
            pl.BlockSpec((1, ec), lambda h, o, j: (0, j)),
        ],
        out_specs=[
            pl.BlockSpec((None, n2, rows, ec), lambda h, o, j: (o, 0, h, j)),
            pl.BlockSpec((None, None, 1, ec), lambda h, o, j: (o, h, 0, j)),
        ],
        out_shape=[
            jax.ShapeDtypeStruct((HY_ORDER, n2, HY_DIRS * rows, e), BF16),
            jax.ShapeDtypeStruct((HY_ORDER, HY_DIRS, 1, e), F32),
        ],
        scratch_shapes=[pltpu.VMEM((l, 3 * fh), BF16)],
        compiler_params=_cparams("arbitrary", "arbitrary", "arbitrary"),
        name="hyena_filter",
    )(feat2, w1p, vec(b_f1), vec(fr1), w_f2, vec(b_f2), vec(fr2), w_f3, vec(b_f3), vec(fr3),
      w_f4, deltas)


FFT_G = BF16_ROWS
FFT_EC = 512


def _fft_a_kernel(z_ref, ta_ref, are_ref, aim_ref):
    g = ta_ref.shape[0]
    n1 = are_ref.shape[0]
    rs = []
    for j in range(g):
        z = z_ref[:, j].reshape(ta_ref.shape[2], z_ref.shape[-1])
        rs.append(jnp.dot(ta_ref[j], z, preferred_element_type=F32).astype(BF16))
    rt = _swap_major_sublane(jnp.stack(rs))
    are_ref[...] = rt[:n1]
    aim_ref[...] = rt[n1:]


def _fft_a(z6, comp, ta, ec=FFT_EC, g=FFT_G):
    _, p, s, n2, r, e = z6.shape
    _, m2, kk = ta.shape
    n1 = m2 // 2
    ec = min(ec, e)
    out = jax.ShapeDtypeStruct((p, n1, n2, e), BF16)
    ospec = pl.BlockSpec((None, n1, g, ec), lambda pi, j, c: (pi, 0, j, c))
    return pl.pallas_call(
        _fft_a_kernel,
        grid=(p, n2 // g, e // ec),
        in_specs=[
            pl.BlockSpec((None, None, s, g, r, ec), lambda pi, j, c: (comp, pi, 0, j, 0, c)),
            pl.BlockSpec((g, m2, kk), lambda pi, j, c: (j, 0, 0)),
        ],
        out_specs=[ospec, ospec],
        out_shape=[out, out],
        compiler_params=_cparams("parallel", "arbitrary", "arbitrary"),
        name="fft_a",
    )(z6, ta)


def _fft_bf_kernel(are_ref, aim_ref, tf_ref, ss_ref, kre_ref, kim_ref):
    g, n2, _ = are_ref.shape
    rs = lax.rsqrt(ss_ref[0] + ss_ref[1] + 1e-12)
    for j in range(g):
        a = jnp.concatenate([are_ref[j], aim_ref[j]], axis=0)
        x = jnp.dot(tf_ref[...], a, preferred_element_type=F32) * rs
        kre_ref[j] = x[:n2].astype(kre_ref.dtype)
        kim_ref[j] = x[n2:].astype(kim_ref.dtype)


def _fft_b_filter(are, aim, tf, ss, ec=FFT_EC, g=FFT_G):
    p, n1, n2, e = are.shape
    ec = min(ec, e)
    spec = pl.BlockSpec((None, g, n2, ec), lambda j, c, pi: (pi, j, 0, c))
    out = jax.ShapeDtypeStruct((p, n1, n2, e), BF16)
    return pl.pallas_call(
        _fft_bf_kernel,
        grid=(n1 // g, e // ec, p),
        in_specs=[spec, spec,
                  pl.BlockSpec((2 * n2, 2 * n2), lambda j, c, pi: (0, 0)),
                  pl.BlockSpec((None, HY_DIRS, 1, ec), lambda j, c, pi: (pi, 0, 0, c))],
        out_specs=[spec, spec],
        out_shape=[out, out],
        compiler_params=_cparams("parallel", "arbitrary", "arbitrary"),
        name="fft_b_filter",
    )(are, aim, tf, ss)


HC_EC = LANE
HC_G = BF16_ROWS
STAGE_B_SPLIT = 1
HC_VMEM_LIMIT = 60 * 1024 * 1024


def _blockdiag(a, b):
    z = jnp.zeros_like(a)
    return jnp.concatenate([jnp.concatenate([a, z], axis=1),
                            jnp.concatenate([z, b], axis=1)], axis=0)


def _hyena_core_kernel(p_ref, wsc_ref, bsc_ref, hb_ref, kre_ref, kim_ref, tap_ref, tfp_ref,
                       tgp_ref, tc_ref, y_ref, a_scr, c_scr, z_scr):
    _, two, n2, h1, ec = p_ref.shape
    n1, g = a_scr.shape[2], a_scr.shape[3]
    rows = two * h1
    ngrp_a, ngrp_b = n2 // g, n1 // g
    half = g // 2
    rowi = lax.broadcasted_iota(jnp.int32, (rows, ec), 0) & (h1 - 1)

    def slab(comp, idx):
        return p_ref[comp, :, idx].reshape(rows, ec).astype(F32)

    def conv_group(comp, gi):
        base = gi * g
        first, last = gi == 0, gi == ngrp_a - 1
        w = [wsc_ref[t, comp:comp + 1, :] for t in range(SHORT_CONV)]
        b = bsc_ref[comp:comp + 1, :]
        prev = slab(comp, jnp.where(first, n2 - 1, base - 1))
        prev = jnp.where(first, jnp.where(rowi == 0, 0.0, pltpu.roll(prev, 1, axis=0)), prev)
        cur = slab(comp, base)
        for j in range(g):
            if j < g - 1:
                nxt = slab(comp, base + j + 1)
            else:
                nxt = slab(comp, jnp.where(last, 0, base + g))
                nxt = jnp.where(last, jnp.where(rowi == h1 - 1, 0.0,
                                                pltpu.roll(nxt, rows - 1, axis=0)), nxt)
            yield b + w[0] * prev + w[1] * cur + w[2] * nxt
            prev, cur = cur, nxt

    def stage_a_group(gi, zs):
        outs = []
        for jj in range(half):
            r = jnp.dot(tap_ref[gi * half + jj], _blockdiag(zs[2 * jj], zs[2 * jj + 1]),
                        preferred_element_type=F32).astype(BF16)
            outs += [r[:, :ec], r[:, ec:]]
        rt = _swap_major_sublane(jnp.stack(outs))
        a_scr[0, gi] = rt[:n1]
        a_scr[1, gi] = rt[n1:]

    def stage_b(order):
        def col(k1):
            return jnp.concatenate([a_scr[0, :, k1].reshape(n2, ec),
                                    a_scr[1, :, k1].reshape(n2, ec)], axis=0)

        def body(kb, carry):
            cs = []
            sub = half // STAGE_B_SPLIT
            for s0 in range(0, half, sub):
                pairs = range(s0, s0 + sub)
                xs = [jnp.dot(tfp_ref[...], _blockdiag(col(kb * g + 2 * jj), col(kb * g + 2 * jj + 1)),
                              preferred_element_type=F32) for jj in pairs]
                rhs = []
                for x, jj in zip(xs, pairs):
                    ka = kb * g + 2 * jj
                    kr = jnp.concatenate([kre_ref[order, ka], kre_ref[order, ka + 1]], axis=1).astype(F32)
                    ki = jnp.concatenate([kim_ref[order, ka], kim_ref[order, ka + 1]], axis=1).astype(F32)
                    xr, xi = x[:n2], x[n2:]
                    yr = (xr * kr - xi * ki).astype(BF16)
                    yi = (xr * ki + xi * kr).astype(BF16)
                    rhs.append(_blockdiag(jnp.concatenate([yr[:, :ec], yi[:, :ec]], axis=0),
                                          jnp.concatenate([yr[:, ec:], yi[:, ec:]], axis=0)))
                for r, jj in zip(rhs, pairs):
                    c = jnp.dot(tgp_ref[kb * half + jj], r,
                                preferred_element_type=F32).astype(BF16)
                    cs += [c[:, :ec], c[:, ec:]]
            ct = _swap_major_sublane(jnp.stack(cs))
            c_scr[0, kb] = ct[:n2]
            c_scr[1, kb] = ct[n2:]
            return carry

        lax.fori_loop(0, ngrp_b, body, 0, unroll=True)

    def stage_c_slab(t1):
        c = jnp.concatenate([c_scr[0, :, t1].reshape(n1, ec), c_scr[1, :, t1].reshape(n1, ec)], axis=0)
        return jnp.dot(tc_ref[...], c, preferred_element_type=F32)

    def body_a1(gi, carry):
        zs = []
        for j, v in enumerate(conv_group(0, gi)):
            zs.append(v.astype(BF16))
            z_scr[gi * g + j] = zs[j]
        stage_a_group(gi, zs)
        return carry

    def body_c1(gi, carry):
        zs = []
        for j, gate in enumerate(conv_group(1, gi)):
            t1 = gi * g + j
            z1 = gate * (stage_c_slab(t1) + hb_ref[0:1, :] * z_scr[t1].astype(F32))
            zs.append(z1.astype(BF16))
            z_scr[t1] = zs[j]
        stage_a_group(gi, zs)
        return carry

    def body_c2(gi, carry):
        for j, gate in enumerate(conv_group(2, gi)):
            t1 = gi * g + j
            z2 = gate * (stage_c_slab(t1) + hb_ref[1:2, :] * z_scr[t1].astype(F32))
            out = z2 * _silu(slab(3, t1))
            y_ref[:, t1] = out.astype(BF16).reshape(two, h1, ec)
        return carry

    lax.fori_loop(0, ngrp_a, body_a1, 0, unroll=True)
    stage_b(0)
    lax.fori_loop(0, ngrp_a, body_c1, 0, unroll=True)
    stage_b(1)
    lax.fori_loop(0, ngrp_a, body_c2, 0, unroll=True)


def _hyena_core(p6, w_sc, b_sc, h_bias, kre, kim, tap, tfp, tgp, tc, ec=HC_EC, g=HC_G):
    _, p, two, n2, h1, e = p6.shape
    n1 = kre.shape[1]
    ec = min(ec, e)
    once = pl.Buffered(1)
    const = lambda shape: pl.BlockSpec(shape, lambda c, pi: (0,) * len(shape), pipeline_mode=once)
    kspec = pl.BlockSpec((HY_ORDER, n1, n2, ec), lambda c, pi: (0, 0, 0, c))
    return pl.pallas_call(
        _hyena_core_kernel,
        grid=(e // ec, p),
        in_specs=[
            pl.BlockSpec((4, None, two, n2, h1, ec), lambda c, pi: (0, pi, 0, 0, 0, c)),
            pl.BlockSpec((SHORT_CONV, 3, ec), lambda c, pi: (0, 0, c)),
            pl.BlockSpec((3, ec), lambda c, pi: (0, c)),
            pl.BlockSpec((HY_ORDER, ec), lambda c, pi: (0, c)),
            kspec, kspec,
            const(tap.shape), const(tfp.shape), const(tgp.shape), const(tc.shape),
        ],
        out_specs=pl.BlockSpec((None, two, n2, h1, ec), lambda c, pi: (pi, 0, 0, 0, c)),
        out_shape=jax.ShapeDtypeStruct((p, two, n2, h1, e), BF16),
        scratch_shapes=[pltpu.VMEM((2, n2 // g, n1, g, ec), BF16),
                        pltpu.VMEM((2, n1 // g, n2, g, ec), BF16),
                        pltpu.VMEM((n2, two * h1, ec), BF16)],
        compiler_params=_cparams("parallel", "arbitrary", vmem_limit=HC_VMEM_LIMIT),
        name="hyena_core",
    )(p6, w_sc.reshape(SHORT_CONV, 3, e), b_sc.reshape(3, e), h_bias, kre, kim, tap, tfp, tgp, tc)


ATT_QB = 8


def _attn_kernel(slope_ref, sink_ref, q_ref, *refs, group):
    nkb = ATT_QB + 2
    k_refs, v_refs = refs[:nkb], refs[nkb:2 * nkb]
    glo_ref, ghi_ref, o_ref = refs[2 * nkb:]
    for s in range(ATT_QB):
        rows = slice(s * WINDOW, (s + 1) * WINDOW)
        _attn_block(slope_ref, sink_ref, q_ref.at[rows], k_refs[s:s + 3], v_refs[s:s + 3],
                    glo_ref.at[rows], ghi_ref.at[rows], o_ref.at[rows],
                    pl.program_id(1) * ATT_QB + s, pl.num_programs(1) * ATT_QB, group)


def _attn_block(slope_ref, sink_ref, q_ref, k_refs, v_refs, glo_ref, ghi_ref, o_ref, n, nb, group):
    kp_ref, kc_ref, kn_ref = k_refs
    vp_ref, vc_ref, vn_ref = v_refs
    blk = q_ref.shape[0]
    hd = HEAD_DIM
    n_kv = kc_ref.shape[1] // hd
    heads_per_g = glo_ref.shape[1] // hd
    nk = 3 * blk
    sj = lax.broadcasted_iota(jnp.int32, (nk, blk), 0)
    qi = lax.broadcasted_iota(jnp.int32, (nk, blk), 1)
    dist = jnp.abs(qi - sj + blk)
    valid = (dist <= WINDOW) & ((sj >= blk) | (n > 0)) & ((sj < 2 * blk) | (n < nb - 1))
    distm = jnp.where(valid, dist.astype(F32), MASK_DIST)
    pad = BF16_ROWS
    first_q = lax.broadcasted_iota(jnp.int32, (pad, blk), 0) == 0
    first_d = lax.broadcasted_iota(jnp.int32, (pad, hd), 0) == 0
    ones_blk = jnp.ones((nk, hd), BF16)
    sink_v = jnp.concatenate([jnp.zeros((pad, hd), BF16),
                              jnp.where(first_d, 1.0, 0.0).astype(BF16)], axis=1)
    for kvh in range(n_kv):
        ksl = slice(kvh * hd, (kvh + 1) * hd)
        k = jnp.concatenate([kp_ref[:, ksl], kc_ref[:, ksl], kn_ref[:, ksl]], axis=0)
        v = jnp.concatenate([vp_ref[:, ksl], vc_ref[:, ksl], vn_ref[:, ksl]], axis=0)
        v_aug = jnp.concatenate([jnp.concatenate([v, ones_blk], axis=1), sink_v], axis=0)
        heads = range(kvh * group, (kvh + 1) * group)
        q = jnp.concatenate([q_ref[:, h * hd:(h + 1) * hd] for h in heads], axis=0)
        st = lax.dot_general(k, q, (((1,), (1,)), ((), ())), preferred_element_type=F32)
        pts = []
        for gi, h in enumerate(heads):
            logit = st[:, gi * blk:(gi + 1) * blk] - slope_ref[h] * distm
            sk = sink_ref[h]
            m = jnp.maximum(jnp.max(logit, axis=0, keepdims=True), sk)
            p = jnp.exp2(logit - m)
            sink_p = jnp.where(first_q, jnp.exp2(sk - m), 0.0)
            pts.append(jnp.concatenate([p, sink_p], axis=0).astype(BF16))
        oa = lax.dot_general(jnp.concatenate(pts, axis=1), v_aug, (((0,), (0,)), ((), ())),
                             preferred_element_type=F32)
        for gi, h in enumerate(heads):
            rows = slice(gi * blk, (gi + 1) * blk)
            o = oa[rows, :hd] * (1.0 / oa[rows, hd:])
            g_half = glo_ref if h < heads_per_g else ghi_ref
            hg = h % heads_per_g
            gate = _silu(g_half[:, hg * hd:(hg + 1) * hd].astype(F32))
            o_ref[:, h * hd:(h + 1) * hd] = (o * gate).astype(o_ref.dtype)


def _attention(proj, slopes, sink, n_heads):
    bsz, l, _ = proj.shape
    hd, blk = HEAD_DIM, WINDOW
    group = n_heads // N_KV_HEADS
    aw = n_heads * hd
    kvw = N_KV_HEADS * hd
    nb = l // blk
    k0 = aw // kvw
    v0 = k0 + 1
    gw = aw // 2
    g0 = (aw + 2 * kvw) // gw
    smem = pl.BlockSpec(memory_space=pltpu.SMEM)
    qb = ATT_QB
    kv = lambda c0, off: pl.BlockSpec(
        (None, blk, kvw), lambda b, n: (b, jnp.clip(qb * n + off, 0, nb - 1), c0))
    kvs = lambda c0: [kv(c0, off) for off in range(-1, qb + 1)]
    gspec = lambda c0: pl.BlockSpec((None, qb * blk, gw), lambda b, n: (b, n, c0))
    assert (aw + 2 * kvw) % gw == 0 and aw % kvw == 0 and nb % qb == 0
    n_in = 3 + 2 * (qb + 2) + 2
    return pl.pallas_call(
        functools.partial(_attn_kernel, group=group),
        grid=(bsz, nb // qb),
        in_specs=[smem, smem,
                  pl.BlockSpec((None, qb * blk, aw), lambda b, n: (b, n, 0)),
                  *kvs(k0), *kvs(v0), gspec(g0), gspec(g0 + 1)],
        out_specs=pl.BlockSpec((None, qb * blk, aw), lambda b, n: (b, n, 0)),
        out_shape=jax.ShapeDtypeStruct((bsz, l, aw), BF16),
        compiler_params=_cparams("parallel", "arbitrary"),
        name="swa_attention",
    )(slopes, sink, *([proj] * (n_in - 2)))


def _hyena_filter_spectrum(l, e, hy, tabs):
    n1 = FFT_N1
    n2 = 2 * l // n1
    k_un, ss = _filters(l, e, n2, hy["w_f1"], hy["b_f1"], hy["fr1"], hy["w_f2"], hy["b_f2"],
                        hy["fr2"], hy["w_f3"], hy["b_f3"], hy["fr3"], hy["w_f4"])
    k6 = k_un.reshape(1, HY_ORDER, 1, n2, n1, e)
    are, aim = _fft_a(k6, 0, tabs["ta_filt"], ec=2 * FFT_EC)
    return _fft_b_filter(are, aim, tabs["tb_fwd"], ss, ec=4 * FFT_EC)


def _hyena_layer(x, hy, kf, tabs, ln_g, ln_b, alpha):
    bsz, l, d = x.shape
    e = hy["w_out"].shape[0]
    kre, kim = kf
    n1 = FFT_N1
    n2 = 2 * l // n1
    h1 = n1 // 2
    proj = _inproj(x.reshape(bsz * l, d), hy["w_in"], hy["b_in"], 4, permute=(l, n2),
                   tn=2 * INPROJ_CHUNK)
    y = _hyena_core(proj.reshape(4, bsz // 2, 2, n2, h1, e), hy["w_sc"], hy["b_sc"], hy["h_bias"],
                    kre, kim, tabs["ta_pair"], tabs["tb_fwd_pair"], tabs["tb_inv_pair"],
                    tabs["tc"])
    out = _outproj_ln_permuted(y.reshape(bsz, n2, h1, e), hy["w_out"], hy["b_out"],
                               x.reshape(bsz, h1, n2, d), ln_g, ln_b, alpha)
    return out.reshape(bsz, l, d)


def _attention_layer(x, at, ln_g, ln_b, alpha):
    bsz, l, d = x.shape
    n_heads = at["sink"].shape[0]
    x2 = x.reshape(bsz * l, d)
    proj = _inproj(x2, at["w_in"], at["b_in"], 1, col_scale=at["col_scale"],
                   tm=512, tn=at["w_in"].shape[1])
    proj = proj.reshape(bsz, l, -1)
    o = _attention(proj, at["slopes"], at["sink"] * LOG2E, n_heads)
    out = _outproj_ln(o.reshape(bsz * l, -1), at["w_out"], at["b_out"], x2, ln_g, ln_b, alpha)
    return out.reshape(bsz, l, d)


def kernel(x_prompt, x_sample, ln_g, ln_b, hy_w_in, hy_b_in, hy_w_sc, hy_b_sc, hy_w_f1, hy_b_f1,
           hy_fr1, hy_w_f2, hy_b_f2, hy_fr2, hy_w_f3, hy_b_f3, hy_fr3, hy_w_f4, hy_h_bias,
           hy_w_out, hy_b_out, at_w_in, at_sink, at_w_out):
    depth = ln_g.shape[0]
    alpha = (2 * depth) ** 0.25
    l = x_prompt.shape[1]
    d = x_prompt.shape[2]
    assert x_sample.shape[1] == l and (2 * l) % FFT_N1 == 0
    assert x_prompt.shape[0] % 2 == 0 and x_sample.shape[0] % 2 == 0
    tabs = {k: jnp.asarray(t).astype(BF16) for k, t in _dft_tables(2 * l).items()}
    row = lambda a: a.reshape(1, -1)

    layers = []
    for i in range(depth):
        j = i // N_MIXERS
        if i % N_MIXERS == 0:
            e = hy_w_out.shape[1]
            hy = dict(w_in=hy_w_in[j].astype(BF16), b_in=row(hy_b_in[j]), w_sc=hy_w_sc[j],
                      b_sc=row(hy_b_sc[j]), w_f1=hy_w_f1[j], b_f1=hy_b_f1[j], fr1=hy_fr1[j],
                      w_f2=hy_w_f2[j], b_f2=hy_b_f2[j], fr2=hy_fr2[j], w_f3=hy_w_f3[j],
                      b_f3=hy_b_f3[j], fr3=hy_fr3[j], w_f4=hy_w_f4[j], h_bias=hy_h_bias[j],
                      w_out=hy_w_out[j].astype(BF16), b_out=row(hy_b_out[j]))
            kf = _hyena_filter_spectrum(l, e, hy, tabs)
            layers.append(("hyena", hy, kf))
        else:
            n_heads = at_sink.shape[1]
            aw = n_heads * HEAD_DIM
            width = at_w_in.shape[2]
            col_scale = jnp.where(jnp.arange(width) < aw, HEAD_DIM ** -0.5 * LOG2E, 1.0)
            slopes = jnp.exp2(-8.0 * jnp.arange(1, n_heads + 1, dtype=F32) / n_heads) * LOG2E
            at = dict(w_in=at_w_in[j].astype(BF16), b_in=jnp.zeros((1, width), F32),
                      col_scale=row(col_scale.astype(F32)), slopes=slopes,
                      sink=at_sink[j], w_out=at_w_out[j].astype(BF16),
                      b_out=jnp.zeros((1, d), F32))
            layers.append(("attn", at, None))

    def trunk(x):
        for i, (kind, prm, kf) in enumerate(layers):
            g, b = row(ln_g[i]), row(ln_b[i])
            if kind == "hyena":
                x = _hyena_layer(x, prm, kf, tabs, g, b, alpha)
            else:
                x = _attention_layer(x, prm, g, b, alpha)
        return x

    return (trunk(x_prompt), trunk(x_sample))
```

```python
import functools
import math

import numpy as np
import jax
import jax.numpy as jnp
from jax import lax
from jax.experimental import pallas as pl
from jax.experimental.pallas import tpu as pltpu

F32 = jnp.float32
BF16 = jnp.bfloat16

N_MIXERS = 2
HY_ORDER = 2
HY_DIRS = 2
SHORT_CONV = 3
POS_EMB_DIM = 33
POS_BANDS = (POS_EMB_DIM - 1) // 2
DECAY_TARGET = 1e-2
FAST_DECAY_PCT = 0.3
SLOW_DECAY_PCT = 1.5
MIN_DECAY = math.log(DECAY_TARGET) / SLOW_DECAY_PCT
MAX_DECAY = math.log(DECAY_TARGET) / FAST_DECAY_PCT
N_KV_HEADS = 4
HEAD_DIM = 128
WINDOW = 128
LN_EPS = 1e-5
LOG2E = 1.4426950408889634
MASK_DIST = 1e30

FFT_N1 = 128
LANE = 128
BF16_ROWS = 16
F32_ROWS = 8
VMEM_LIMIT = 56 * 1024 * 1024


def _cparams(*sem, vmem_limit=VMEM_LIMIT):
    return pltpu.CompilerParams(dimension_semantics=sem, vmem_limit_bytes=vmem_limit)


def _swap_major_sublane(x):
    return jnp.swapaxes(x, 0, 1)


def _silu(x):
    return x * (0.5 * jnp.tanh(0.5 * x) + 0.5)


def _block(m):
    return np.block([[m.real, -m.imag], [m.imag, m.real]])


@functools.lru_cache(maxsize=None)
def _dft_tables(n):
    n1, n2 = FFT_N1, n // FFT_N1
    h1 = n1 // 2
    k1 = np.arange(n1)[:, None]
    ta_data, ta_filt = [], []
    for j in range(n2):
        m = np.exp(-2j * np.pi * (j * k1 / n + np.arange(n1)[None, :] * k1 / n1))
        ta_data.append(_block(m[:, :h1]))
        ta_filt.append(np.concatenate([m.real, m.imag], 0))
    f2 = np.exp(-2j * np.pi * np.arange(n2)[:, None] * np.arange(n2)[None, :] / n2)
    tb_fwd = _block(f2)
    t1 = np.arange(n2)[:, None]
    tb_inv = [_block(np.exp(2j * np.pi * (kk * t1 / n + np.arange(n2)[None, :] * t1 / n2)))
              for kk in range(n1)]
    hc = np.exp(2j * np.pi * np.arange(h1)[:, None] * np.arange(n1)[None, :] / n1) / n
    tc = _block(hc)
    f = lambda a: np.asarray(a, np.float32)
    pair = lambda t: np.concatenate([t[0::2], t[1::2]], axis=2)
    ta_data, tb_inv = np.stack(ta_data), np.stack(tb_inv)
    return dict(ta_filt=f(np.stack(ta_filt)), tb_fwd=f(tb_fwd), tc=f(tc),
                ta_pair=f(pair(ta_data)), tb_inv_pair=f(pair(tb_inv)),
                tb_fwd_pair=f(np.concatenate([tb_fwd, tb_fwd], axis=1)))


INPROJ_CHUNK = 1024


def _inproj_kernel(x_ref, w_ref, b_ref, cs_ref, o_ref, xb_ref, *, scaled):
    @pl.when(pl.program_id(1) == 0)
    def _():
        xb_ref[...] = x_ref[...].astype(BF16)

    tn = w_ref.shape[1]
    cw = min(tn, INPROJ_CHUNK)
    for c in range(tn // cw):
        sl = slice(c * cw, (c + 1) * cw)
        acc = jnp.dot(xb_ref[...], w_ref[:, sl], preferred_element_type=F32)
        if scaled:
            acc = acc * cs_ref[:, sl]
        val = (acc + b_ref[:, sl]).astype(o_ref.dtype)
        if len(o_ref.shape) == 3:
            n2, n1b, _ = o_ref.shape
            o_ref[:, :, sl] = _swap_major_sublane(val.reshape(n1b, n2, cw))
        else:
            o_ref[:, sl] = val


def _inproj(x, w, b, n_comp, col_scale=None, permute=None, tm=1024, tn=1024):
    m, k = x.shape
    n = w.shape[1]
    wc = n // n_comp
    tn = min(tn, wc)
    ncb = wc // tn
    scaled = col_scale is not None
    if not scaled:
        col_scale = b
    if permute is None:
        out_spec = pl.BlockSpec((None, tm, tn), lambda i, j: (j // ncb, i, j % ncb))
        out_shape = jax.ShapeDtypeStruct((n_comp, m, wc), BF16)
    else:
        l, n2 = permute
        tm = BF16_ROWS * n2
        tpb = l // tm
        out_spec = pl.BlockSpec((None, None, n2, BF16_ROWS, tn),
                                lambda i, j: (j // ncb, i // tpb, 0, i % tpb, j % ncb))
        out_shape = jax.ShapeDtypeStruct((n_comp, m // l, n2, l // n2, wc), BF16)
    vec = pl.BlockSpec((1, tn), lambda i, j: (0, j))
    w_mode = {"pipeline_mode": pl.Buffered(1)} if tn == n else {}
    return pl.pallas_call(
        functools.partial(_inproj_kernel, scaled=scaled),
        grid=(m // tm, n // tn),
        in_specs=[
            pl.BlockSpec((tm, k), lambda i, j: (i, 0)),
            pl.BlockSpec((k, tn), lambda i, j: (0, j), **w_mode),
            vec, vec,
        ],
        out_specs=out_spec,
        out_shape=out_shape,
        scratch_shapes=[pltpu.VMEM((tm, k), BF16)],
        compiler_params=_cparams("parallel", "arbitrary"),
        name="inproj",
    )(x, w, b, col_scale)


def _outproj_ln_kernel(y_ref, w_ref, b_ref, x_ref, g_ref, beta_ref, o_ref, *, alpha, permuted):
    y = y_ref[...]
    if permuted:
        nb, na, kk = y.shape
        y = y.reshape(nb * na, kk)
    h = jnp.dot(y, w_ref[...], preferred_element_type=F32) + b_ref[...]
    if permuted:
        h = _swap_major_sublane(h.reshape(nb, na, h.shape[-1]))
    r = alpha * x_ref[...] + h
    mu = jnp.mean(r, axis=-1, keepdims=True)
    d = r - mu
    var = jnp.mean(d * d, axis=-1, keepdims=True)
    o_ref[...] = d * lax.rsqrt(var + LN_EPS) * g_ref[...] + beta_ref[...]


def _outproj_ln(y, w, b, x, g, beta, alpha, tm=512):
    m, k = y.shape
    d = w.shape[1]
    row = lambda i: (i, 0)
    fixed = lambda i: (0, 0)
    return pl.pallas_call(
        functools.partial(_outproj_ln_kernel, alpha=alpha, permuted=False),
        grid=(m // tm,),
        in_specs=[
            pl.BlockSpec((tm, k), row),
            pl.BlockSpec((k, d), fixed),
            pl.BlockSpec((1, d), fixed),
            pl.BlockSpec((tm, d), row),
            pl.BlockSpec((1, d), fixed),
            pl.BlockSpec((1, d), fixed),
        ],
        out_specs=pl.BlockSpec((tm, d), row),
        out_shape=jax.ShapeDtypeStruct((m, d), F32),
        compiler_params=_cparams("parallel"),
        name="outproj_ln",
    )(y, w, b, x, g, beta)


def _outproj_ln_permuted(y, w, b, x, g, beta, alpha):
    bsz, n2, h1, k = y.shape
    d = w.shape[1]
    nb = F32_ROWS
    fixed = lambda bi, j: (0, 0)
    xspec = pl.BlockSpec((None, h1, nb, d), lambda bi, j: (bi, 0, j, 0))
    return pl.pallas_call(
        functools.partial(_outproj_ln_kernel, alpha=alpha, permuted=True),
        grid=(bsz, n2 // nb),
        in_specs=[
            pl.BlockSpec((None, nb, h1, k), lambda bi, j: (bi, j, 0, 0)),
            pl.BlockSpec((k, d), fixed),
            pl.BlockSpec((1, d), fixed),
            xspec,
            pl.BlockSpec((1, d), fixed),
            pl.BlockSpec((1, d), fixed),
        ],
        out_specs=xspec,
        out_shape=jax.ShapeDtypeStruct((bsz, h1, n2, d), F32),
        compiler_params=_cparams("parallel", "parallel"),
        name="outproj_ln_perm",
    )(y, w, b, x, g, beta)


def _filter_kernel(feat_ref, w1_ref, b1_ref, fr1_ref, w2_ref, b2_ref, fr2_ref, w3_ref, b3_ref,
                   fr3_ref, w4_ref, delta_ref, k_ref, ss_ref, h_ref):
    half = pl.program_id(0)
    hp = lax.Precision.HIGHEST

    @pl.when((pl.program_id(1) == 0) & (pl.program_id(2) == 0))
    def _():
        h = jnp.sin(fr1_ref[...] * (jnp.dot(feat_ref[...], w1_ref[...], precision=hp,
                                             preferred_element_type=F32) + b1_ref[...]))
        h = jnp.sin(fr2_ref[...] * (jnp.dot(h, w2_ref[...], precision=hp,
                                             preferred_element_type=F32) + b2_ref[...]))
        h = jnp.sin(fr3_ref[...] * (jnp.dot(h, w3_ref[...], precision=hp,
                                             preferred_element_type=F32) + b3_ref[...]))
        h_hi = h.astype(BF16)
        h_lo = (h - h_hi.astype(F32)).astype(BF16)
        h_ref[...] = jnp.concatenate([h_hi, h_hi, h_lo], axis=1)

    w4 = w4_ref[...]
    w_hi = w4.astype(BF16)
    w_lo = (w4 - w_hi.astype(F32)).astype(BF16)
    k = jnp.dot(h_ref[...], jnp.concatenate([w_hi, w_lo, w_hi], axis=0), preferred_element_type=F32)
    t_norm = feat_ref[:, 0:1]
    k = k * jnp.exp(-t_norm * delta_ref[...])
    row = lax.broadcasted_iota(jnp.int32, k.shape, 0)
    k = jnp.where((row == 0) & (half == 1), 0.0, k)
    ss_ref[...] = jnp.sum(k * k, axis=0, keepdims=True)
    n2, rows, ec = k_ref.shape
    k_ref[...] = _swap_major_sublane(k.astype(k_ref.dtype).reshape(rows, n2, ec))


def _filters(l, e, n2, w_f1, b_f1, fr1, w_f2, b_f2, fr2, w_f3, b_f3, fr3, w_f4, ec=512):
    fh = w_f1.shape[1]
    ec = min(ec, e)
    t_norm = jnp.linspace(0.0, 1.0, l, dtype=F32)
    w = 2.0 * math.pi * jnp.arange(l, dtype=F32) / l
    f = jnp.linspace(1e-4, POS_BANDS - 1, POS_BANDS, dtype=F32)
    ang = w[:, None] * f[None, :]
    feat = jnp.concatenate([t_norm[:, None], jnp.cos(ang), -jnp.sin(ang)], axis=-1)
    feat_rev = jnp.concatenate([feat[:1], feat[:0:-1]], axis=0)
    feat2 = jnp.pad(jnp.concatenate([feat, feat_rev], axis=0), ((0, 0), (0, LANE - POS_EMB_DIM)))
    w1p = jnp.pad(w_f1, ((0, LANE - POS_EMB_DIM), (0, 0)))
    deltas = jnp.abs(jnp.linspace(MIN_DECAY, MAX_DECAY, e, dtype=F32))[None, :]
    nec = e // ec
    rows = l // n2
    vec = lambda a: a.reshape(1, -1)
    fixed = lambda h, o, j: (0, 0)
    return pl.pallas_call(
        _filter_kernel,
        grid=(HY_DIRS, HY_ORDER, nec),
        in_specs=[
            pl.BlockSpec((l, LANE), lambda h, o, j: (h, 0)),
            pl.BlockSpec((LANE, fh), fixed), pl.BlockSpec((1, fh), fixed), pl.BlockSpec((1, fh), fixed),
            pl.BlockSpec((fh, fh), fixed), pl.BlockSpec((1, fh), fixed), pl.BlockSpec((1, fh), fixed),
            pl.BlockSpec((fh, fh), fixed), pl.BlockSpec((1, fh), fixed), pl.BlockSpec((1, fh), fixed),
            pl.BlockSpec((fh, ec), lambda h, o, j: (0, (o * HY_DIRS + h) * nec + j)),
            pl.BlockSpec((1, ec), lambda h, o, j: (0, j)),
        ],
        out_specs=[
            pl.BlockSpec((None, n2, rows, ec), lambda h, o, j: (o, 0, h, j)),
            pl.BlockSpec((None, None, 1, ec), lambda h, o, j: (o, h, 0, j)),
        ],
        out_shape=[
            jax.ShapeDtypeStruct((HY_ORDER, n2, HY_DIRS * rows, e), BF16),
            jax.ShapeDtypeStruct((HY_ORDER, HY_DIRS, 1, e), F32),
        ],
        scratch_shapes=[pltpu.VMEM((l, 3 * fh), BF16)],
        compiler_params=_cparams("arbitrary", "arbitrary", "arbitrary"),
        name="hyena_filter",
    )(feat2, w1p, vec(b_f1), vec(fr1), w_f2, vec(b_f2), vec(fr2), w_f3, vec(b_f3), vec(fr3),
      w_f4, deltas)


FFT_G = BF16_ROWS
FFT_EC = 512


def _fft_a_kernel(z_ref, ta_ref, are_ref, aim_ref):
    g = ta_ref.shape[0]
    n1 = are_ref.shape[0]
    rs = []
    for j in range(g):
        z = z_ref[:, j].reshape(ta_ref.shape[2], z_ref.shape[-1])
        rs.append(jnp.dot(ta_ref[j], z, preferred_element_type=F32).astype(BF16))
    rt = _swap_major_sublane(jnp.stack(rs))
    are_ref[...] = rt[:n1]
    aim_ref[...] = rt[n1:]


def _fft_a(z6, comp, ta, ec=FFT_EC, g=FFT_G):
    _, p, s, n2, r, e = z6.shape
    _, m2, kk = ta.shape
    n1 = m2 // 2
    ec = min(ec, e)
    out = jax.ShapeDtypeStruct((p, n1, n2, e), BF16)
    ospec = pl.BlockSpec((None, n1, g, ec), lambda pi, j, c: (pi, 0, j, c))
    return pl.pallas_call(
        _fft_a_kernel,
        grid=(p, n2 // g, e // ec),
        in_specs=[
            pl.BlockSpec((None, None, s, g, r, ec), lambda pi, j, c: (comp, pi, 0, j, 0, c)),
            pl.BlockSpec((g, m2, kk), lambda pi, j, c: (j, 0, 0)),
        ],
        out_specs=[ospec, ospec],
        out_shape=[out, out],
        compiler_params=_cparams("parallel", "arbitrary", "arbitrary"),
        name="fft_a",
    )(z6, ta)


def _fft_bf_kernel(are_ref, aim_ref, tf_ref, ss_ref, kre_ref, kim_ref):
    g, n2, _ = are_ref.shape
    rs = lax.rsqrt(ss_ref[0] + ss_ref[1] + 1e-12)
    for j in range(g):
        a = jnp.concatenate([are_ref[j], aim_ref[j]], axis=0)
        x = jnp.dot(tf_ref[...], a, preferred_element_type=F32) * rs
        kre_ref[j] = x[:n2].astype(kre_ref.dtype)
        kim_ref[j] = x[n2:].astype(kim_ref.dtype)


def _fft_b_filter(are, aim, tf, ss, ec=FFT_EC, g=FFT_G):
    p, n1, n2, e = are.shape
    ec = min(ec, e)
    spec = pl.BlockSpec((None, g, n2, ec), lambda j, c, pi: (pi, j, 0, c))
    out = jax.ShapeDtypeStruct((p, n1, n2, e), BF16)
    return pl.pallas_call(
        _fft_bf_kernel,
        grid=(n1 // g, e // ec, p),
        in_specs=[spec, spec,
                  pl.BlockSpec((2 * n2, 2 * n2), lambda j, c, pi: (0, 0)),
                  pl.BlockSpec((None, HY_DIRS, 1, ec), lambda j, c, pi: (pi, 0, 0, c))],
        out_specs=[spec, spec],
        out_shape=[out, out],
        compiler_params=_cparams("parallel", "arbitrary", "arbitrary"),
        name="fft_b_filter",
    )(are, aim, tf, ss)


HC_EC = LANE
HC_G = BF16_ROWS
STAGE_B_SPLIT = 1
HC_VMEM_LIMIT = 60 * 1024 * 1024


def _blockdiag(a, b):
    z = jnp.zeros_like(a)
    return jnp.concatenate([jnp.concatenate([a, z], axis=1),
                            jnp.concatenate([z, b], axis=1)], axis=0)


def _hyena_core_kernel(p_ref, wsc_ref, bsc_ref, hb_ref, kre_ref, kim_ref, tap_ref, tfp_ref,
                       tgp_ref, tc_ref, y_ref, a_scr, c_scr, z_scr):
    _, two, n2, h1, ec = p_ref.shape
    n1, g = a_scr.shape[2], a_scr.shape[3]
    rows = two * h1
    ngrp_a, ngrp_b = n2 // g, n1 // g
    half = g // 2
    rowi = lax.broadcasted_iota(jnp.int32, (rows, ec), 0) & (h1 - 1)

    def slab(comp, idx):
        return p_ref[comp, :, idx].reshape(rows, ec).astype(F32)

    def conv_group(comp, gi):
        base = gi * g
        first, last = gi == 0, gi == ngrp_a - 1
        w = [wsc_ref[t, comp:comp + 1, :].astype(BF16) for t in range(SHORT_CONV)]
        b = bsc_ref[comp:comp + 1, :].astype(BF16)
        raw = lambda idx: p_ref[comp, :, idx].reshape(rows, ec)
        prev = slab(comp, jnp.where(first, n2 - 1, base - 1))
        prev = jnp.where(first, jnp.where(rowi == 0, 0.0, pltpu.roll(prev, 1, axis=0)), prev)
        prev = prev.astype(BF16)
        cur = raw(base)
        for j in range(g):
            if j < g - 1:
                nxt = raw(base + j + 1)
            else:
                nxt = slab(comp, jnp.where(last, 0, base + g))
                nxt = jnp.where(last, jnp.where(rowi == h1 - 1, 0.0,
                                                pltpu.roll(nxt, rows - 1, axis=0)), nxt).astype(BF16)
            yield b + w[0] * prev + w[1] * cur + w[2] * nxt
            prev, cur = cur, nxt

    def stage_a_group(gi, zs):
        outs = []
        for jj in range(half):
            r = jnp.dot(tap_ref[gi * half + jj], _blockdiag(zs[2 * jj], zs[2 * jj + 1]),
                        preferred_element_type=F32).astype(BF16)
            outs += [r[:, :ec], r[:, ec:]]
        rt = _swap_major_sublane(jnp.stack(outs))
        a_scr[0, gi] = rt[:n1]
        a_scr[1, gi] = rt[n1:]

    def stage_b(order):
        def col(k1):
            return jnp.concatenate([a_scr[0, :, k1].reshape(n2, ec),
                                    a_scr[1, :, k1].reshape(n2, ec)], axis=0)

        def body(kb, carry):
            cs = []
            sub = half // STAGE_B_SPLIT
            for s0 in range(0, half, sub):
                pairs = range(s0, s0 + sub)
                xs = [jnp.dot(tfp_ref[...], _blockdiag(col(kb * g + 2 * jj), col(kb * g + 2 * jj + 1)),
                              preferred_element_type=F32) for jj in pairs]
                rhs = []
                for x, jj in zip(xs, pairs):
                    ka = kb * g + 2 * jj
                    kr = jnp.concatenate([kre_ref[order, ka], kre_ref[order, ka + 1]], axis=1).astype(F32)
                    ki = jnp.concatenate([kim_ref[order, ka], kim_ref[order, ka + 1]], axis=1).astype(F32)
                    xr, xi = x[:n2], x[n2:]
                    yr = (xr * kr - xi * ki).astype(BF16)
                    yi = (xr * ki + xi * kr).astype(BF16)
                    rhs.append(_blockdiag(jnp.concatenate([yr[:, :ec], yi[:, :ec]], axis=0),
                                          jnp.concatenate([yr[:, ec:], yi[:, ec:]], axis=0)))
                for r, jj in zip(rhs, pairs):
                    c = jnp.dot(tgp_ref[kb * half + jj], r,
                                preferred_element_type=F32).astype(BF16)
                    cs += [c[:, :ec], c[:, ec:]]
            ct = _swap_major_sublane(jnp.stack(cs))
            c_scr[0, kb] = ct[:n2]
            c_scr[1, kb] = ct[n2:]
            return carry

        lax.fori_loop(0, ngrp_b, body, 0, unroll=True)

    def stage_c_slab(t1):
        c = jnp.concatenate([c_scr[0, :, t1].reshape(n1, ec), c_scr[1, :, t1].reshape(n1, ec)], axis=0)
        return jnp.dot(tc_ref[...], c, preferred_element_type=F32)

    def body_a1(gi, carry):
        zs = []
        for j, v in enumerate(conv_group(0, gi)):
            zs.append(v.astype(BF16))
            z_scr[gi * g + j] = zs[j]
        stage_a_group(gi, zs)
        return carry

    def body_c1(gi, carry):
        zs = []
        for j, gate in enumerate(conv_group(1, gi)):
            t1 = gi * g + j
            z1 = gate * (stage_c_slab(t1) + hb_ref[0:1, :] * z_scr[t1].astype(F32))
            zs.append(z1.astype(BF16))
            z_scr[t1] = zs[j]
        stage_a_group(gi, zs)
        return carry

    def body_c2(gi, carry):
        for j, gate in enumerate(conv_group(2, gi)):
            t1 = gi * g + j
            z2 = gate * (stage_c_slab(t1) + hb_ref[1:2, :] * z_scr[t1].astype(F32))
            out = z2 * _silu(slab(3, t1))
            y_ref[:, t1] = out.astype(BF16).reshape(two, h1, ec)
        return carry

    lax.fori_loop(0, ngrp_a, body_a1, 0, unroll=True)
    stage_b(0)
    lax.fori_loop(0, ngrp_a, body_c1, 0, unroll=True)
    stage_b(1)
    lax.fori_loop(0, ngrp_a, body_c2, 0, unroll=True)


def _hyena_core(p6, w_sc, b_sc, h_bias, kre, kim, tap, tfp, tgp, tc, ec=HC_EC, g=HC_G):
    _, p, two, n2, h1, e = p6.shape
    n1 = kre.shape[1]
    ec = min(ec, e)
    once = pl.Buffered(1)
    const = lambda shape: pl.BlockSpec(shape, lambda c, pi: (0,) * len(shape), pipeline_mode=once)
    kspec = pl.BlockSpec((HY_ORDER, n1, n2, ec), lambda c, pi: (0, 0, 0, c))
    return pl.pallas_call(
        _hyena_core_kernel,
        grid=(e // ec, p),
        in_specs=[
            pl.BlockSpec((4, None, two, n2, h1, ec), lambda c, pi: (0, pi, 0, 0, 0, c)),
            pl.BlockSpec((SHORT_CONV, 3, ec), lambda c, pi: (0, 0, c)),
            pl.BlockSpec((3, ec), lambda c, pi: (0, c)),
            pl.BlockSpec((HY_ORDER, ec), lambda c, pi: (0, c)),
            kspec, kspec,
            const(tap.shape), const(tfp.shape), const(tgp.shape), const(tc.shape),
        ],
        out_specs=pl.BlockSpec((None, two, n2, h1, ec), lambda c, pi: (pi, 0, 0, 0, c)),
        out_shape=jax.ShapeDtypeStruct((p, two, n2, h1, e), BF16),
        scratch_shapes=[pltpu.VMEM((2, n2 // g, n1, g, ec), BF16),
                        pltpu.VMEM((2, n1 // g, n2, g, ec), BF16),
                        pltpu.VMEM((n2, two * h1, ec), BF16)],
        compiler_params=_cparams("parallel", "arbitrary", vmem_limit=HC_VMEM_LIMIT),
        name="hyena_core",
    )(p6, w_sc.reshape(SHORT_CONV, 3, e), b_sc.reshape(3, e), h_bias, kre, kim, tap, tfp, tgp, tc)


ATT_QB = 8


def _attn_kernel(slope_ref, sink_ref, q_ref, *refs, group):
    nkb = ATT_QB + 2
    k_refs, v_refs = refs[:nkb], refs[nkb:2 * nkb]
    glo_ref, ghi_ref, o_ref = refs[2 * nkb:]
    for s in range(ATT_QB):
        rows = slice(s * WINDOW, (s + 1) * WINDOW)
        _attn_block(slope_ref, sink_ref, q_ref.at[rows], k_refs[s:s + 3], v_refs[s:s + 3],
                    glo_ref.at[rows], ghi_ref.at[rows], o_ref.at[rows],
                    pl.program_id(1) * ATT_QB + s, pl.num_programs(1) * ATT_QB, group)


def _attn_block(slope_ref, sink_ref, q_ref, k_refs, v_refs, glo_ref, ghi_ref, o_ref, n, nb, group):
    kp_ref, kc_ref, kn_ref = k_refs
    vp_ref, vc_ref, vn_ref = v_refs
    blk = q_ref.shape[0]
    hd = HEAD_DIM
    n_kv = kc_ref.shape[1] // hd
    heads_per_g = glo_ref.shape[1] // hd
    nk = 3 * blk
    sj = lax.broadcasted_iota(jnp.int32, (nk, blk), 0)
    qi = lax.broadcasted_iota(jnp.int32, (nk, blk), 1)
    dist = jnp.abs(qi - sj + blk)
    valid = (dist <= WINDOW) & ((sj >= blk) | (n > 0)) & ((sj < 2 * blk) | (n < nb - 1))
    distm = jnp.where(valid, dist.astype(F32), MASK_DIST)
    pad = BF16_ROWS
    first_q = lax.broadcasted_iota(jnp.int32, (pad, blk), 0) == 0
    first_d = lax.broadcasted_iota(jnp.int32, (pad, hd), 0) == 0
    ones_blk = jnp.ones((nk, hd), BF16)
    sink_v = jnp.concatenate([jnp.zeros((pad, hd), BF16),
                              jnp.where(first_d, 1.0, 0.0).astype(BF16)], axis=1)
    for kvh in range(n_kv):
        ksl = slice(kvh * hd, (kvh + 1) * hd)
        k = jnp.concatenate([kp_ref[:, ksl], kc_ref[:, ksl], kn_ref[:, ksl]], axis=0)
        v = jnp.concatenate([vp_ref[:, ksl], vc_ref[:, ksl], vn_ref[:, ksl]], axis=0)
        v_aug = jnp.concatenate([jnp.concatenate([v, ones_blk], axis=1), sink_v], axis=0)
        heads = range(kvh * group, (kvh + 1) * group)
        q = jnp.concatenate([q_ref[:, h * hd:(h + 1) * hd] for h in heads], axis=0)
        st = lax.dot_general(k, q, (((1,), (1,)), ((), ())), preferred_element_type=F32)
        pts = []
        for gi, h in enumerate(heads):
            logit = st[:, gi * blk:(gi + 1) * blk] - slope_ref[h] * distm
            sk = sink_ref[h]
            m = jnp.maximum(jnp.max(logit, axis=0, keepdims=True), sk)
            p = jnp.exp2(logit - m)
            sink_p = jnp.where(first_q, jnp.exp2(sk - m), 0.0)
            pts.append(jnp.concatenate([p, sink_p], axis=0).astype(BF16))
        oa = lax.dot_general(jnp.concatenate(pts, axis=1), v_aug, (((0,), (0,)), ((), ())),
                             preferred_element_type=F32)
        for gi, h in enumerate(heads):
            rows = slice(gi * blk, (gi + 1) * blk)
            o = oa[rows, :hd] * (1.0 / oa[rows, hd:])
            g_half = glo_ref if h < heads_per_g else ghi_ref
            hg = h % heads_per_g
            gate = _silu(g_half[:, hg * hd:(hg + 1) * hd].astype(F32))
            o_ref[:, h * hd:(h + 1) * hd] = (o * gate).astype(o_ref.dtype)


def _attention(proj, slopes, sink, n_heads):
    bsz, l, _ = proj.shape
    hd, blk = HEAD_DIM, WINDOW
    group = n_heads // N_KV_HEADS
    aw = n_heads * hd
    kvw = N_KV_HEADS * hd
    nb = l // blk
    k0 = aw // kvw
    v0 = k0 + 1
    gw = aw // 2
    g0 = (aw + 2 * kvw) // gw
    smem = pl.BlockSpec(memory_space=pltpu.SMEM)
    qb = ATT_QB
    kv = lambda c0, off: pl.BlockSpec(
        (None, blk, kvw), lambda b, n: (b, jnp.clip(qb * n + off, 0, nb - 1), c0))
    kvs = lambda c0: [kv(c0, off) for off in range(-1, qb + 1)]
    gspec = lambda c0: pl.BlockSpec((None, qb * blk, gw), lambda b, n: (b, n, c0))
    assert (aw + 2 * kvw) % gw == 0 and aw % kvw == 0 and nb % qb == 0
    n_in = 3 + 2 * (qb + 2) + 2
    return pl.pallas_call(
        functools.partial(_attn_kernel, group=group),
        grid=(bsz, nb // qb),
        in_specs=[smem, smem,
                  pl.BlockSpec((None, qb * blk, aw), lambda b, n: (b, n, 0)),
                  *kvs(k0), *kvs(v0), gspec(g0), gspec(g0 + 1)],
        out_specs=pl.BlockSpec((None, qb * blk, aw), lambda b, n: (b, n, 0)),
        out_shape=jax.ShapeDtypeStruct((bsz, l, aw), BF16),
        compiler_params=_cparams("parallel", "arbitrary"),
        name="swa_attention",
    )(slopes, sink, *([proj] * (n_in - 2)))


def _hyena_filter_spectrum(l, e, hy, tabs):
    n1 = FFT_N1
    n2 = 2 * l // n1
    k_un, ss = _filters(l, e, n2, hy["w_f1"], hy["b_f1"], hy["fr1"], hy["w_f2"], hy["b_f2"],
                        hy["fr2"], hy["w_f3"], hy["b_f3"], hy["fr3"], hy["w_f4"])
    k6 = k_un.reshape(1, HY_ORDER, 1, n2, n1, e)
    are, aim = _fft_a(k6, 0, tabs["ta_filt"], ec=2 * FFT_EC)
    return _fft_b_filter(are, aim, tabs["tb_fwd"], ss, ec=4 * FFT_EC)


def _hyena_layer(x, hy, kf, tabs, ln_g, ln_b, alpha):
    bsz, l, d = x.shape
    e = hy["w_out"].shape[0]
    kre, kim = kf
    n1 = FFT_N1
    n2 = 2 * l // n1
    h1 = n1 // 2
    proj = _inproj(x.reshape(bsz * l, d), hy["w_in"], hy["b_in"], 4, permute=(l, n2),
                   tn=2 * INPROJ_CHUNK)
    y = _hyena_core(proj.reshape(4, bsz // 2, 2, n2, h1, e), hy["w_sc"], hy["b_sc"], hy["h_bias"],
                    kre, kim, tabs["ta_pair"], tabs["tb_fwd_pair"], tabs["tb_inv_pair"],
                    tabs["tc"])
    out = _outproj_ln_permuted(y.reshape(bsz, n2, h1, e), hy["w_out"], hy["b_out"],
                               x.reshape(bsz, h1, n2, d), ln_g, ln_b, alpha)
    return out.reshape(bsz, l, d)


def _attention_layer(x, at, ln_g, ln_b, alpha):
    bsz, l, d = x.shape
    n_heads = at["sink"].shape[0]
    x2 = x.reshape(bsz * l, d)
    proj = _inproj(x2, at["w_in"], at["b_in"], 1, col_scale=at["col_scale"],
                   tm=512, tn=at["w_in"].shape[1])
    proj = proj.reshape(bsz, l, -1)
    o = _attention(proj, at["slopes"], at["sink"] * LOG2E, n_heads)
    out = _outproj_ln(o.reshape(bsz * l, -1), at["w_out"], at["b_out"], x2, ln_g, ln_b, alpha)
    return out.reshape(bsz, l, d)


def kernel(x_prompt, x_sample, ln_g, ln_b, hy_w_in, hy_b_in, hy_w_sc, hy_b_sc, hy_w_f1, hy_b_f1,
           hy_fr1, hy_w_f2, hy_b_f2, hy_fr2, hy_w_f3, hy_b_f3, hy_fr3, hy_w_f4, hy_h_bias,
           hy_w_out, hy_b_out, at_w_in, at_sink, at_w_out):
    depth = ln_g.shape[0]
    alpha = (2 * depth) ** 0.25
    l = x_prompt.shape[1]
    d = x_prompt.shape[2]
    assert x_sample.shape[1] == l and (2 * l) % FFT_N1 == 0
    assert x_prompt.shape[0] % 2 == 0 and x_sample.shape[0] % 2 == 0
    tabs = {k: jnp.asarray(t).astype(BF16) for k, t in _dft_tables(2 * l).items()}
    row = lambda a: a.reshape(1, -1)

    layers = []
    for i in range(depth):
        j = i // N_MIXERS
        if i % N_MIXERS == 0:
            e = hy_w_out.shape[1]
            hy = dict(w_in=hy_w_in[j].astype(BF16), b_in=row(hy_b_in[j]), w_sc=hy_w_sc[j],
                      b_sc=row(hy_b_sc[j]), w_f1=hy_w_f1[j], b_f1=hy_b_f1[j], fr1=hy_fr1[j],
                      w_f2=hy_w_f2[j], b_f2=hy_b_f2[j], fr2=hy_fr2[j], w_f3=hy_w_f3[j],
                      b_f3=hy_b_f3[j], fr3=hy_fr3[j], w_f4=hy_w_f4[j], h_bias=hy_h_bias[j],
                      w_out=hy_w_out[j].astype(BF16), b_out=row(hy_b_out[j]))
            kf = _hyena_filter_spectrum(l, e, hy, tabs)
            layers.append(("hyena", hy, kf))
        else:
            n_heads = at_sink.shape[1]
            aw = n_heads * HEAD_DIM
            width = at_w_in.shape[2]
            col_scale = jnp.where(jnp.arange(width) < aw, HEAD_DIM ** -0.5 * LOG2E, 1.0)
            slopes = jnp.exp2(-8.0 * jnp.arange(1, n_heads + 1, dtype=F32) / n_heads) * LOG2E
            at = dict(w_in=at_w_in[j].astype(BF16), b_in=jnp.zeros((1, width), F32),
                      col_scale=row(col_scale.astype(F32)), slopes=slopes,
                      sink=at_sink[j], w_out=at_w_out[j].astype(BF16),
                      b_out=jnp.zeros((1, d), F32))
            layers.append(("attn", at, None))

    def trunk(x):
        for i, (kind, prm, kf) in enumerate(layers):
            g, b = row(ln_g[i]), row(ln_b[i])
            if kind == "hyena":
                x = _hyena_layer(x, prm, kf, tabs, g, b, alpha)
            else:
                x = _attention_layer(x, prm, g, b, alpha)
        return x

    return (trunk(x_prompt), trunk(x_sample))
```

```python
import functools
import math

import numpy as np
import jax
import jax.numpy as jnp
from jax import lax
from jax.experimental import pallas as pl
from jax.experimental.pallas import tpu as pltpu

F32 = jnp.float32
BF16 = jnp.bfloat16

N_MIXERS = 2
HY_ORDER = 2
HY_DIRS = 2
SHORT_CONV = 3
POS_EMB_DIM = 33
POS_BANDS = (POS_EMB_DIM - 1) // 2
DECAY_TARGET = 1e-2
FAST_DECAY_PCT = 0.3
SLOW_DECAY_PCT = 1.5
MIN_DECAY = math.log(DECAY_TARGET) / SLOW_DECAY_PCT
MAX_DECAY = math.log(DECAY_TARGET) / FAST_DECAY_PCT
N_KV_HEADS = 4
HEAD_DIM = 128
WINDOW = 128
LN_EPS = 1e-5
LOG2E = 1.4426950408889634
MASK_DIST = 1e30

FFT_N1 = 128
LANE = 128
BF16_ROWS = 16
F32_ROWS = 8
VMEM_LIMIT = 56 * 1024 * 1024


def _cparams(*sem, vmem_limit=VMEM_LIMIT):
    return pltpu.CompilerParams(dimension_semantics=sem, vmem_limit_bytes=vmem_limit)


def _swap_major_sublane(x):
    return jnp.swapaxes(x, 0, 1)


def _silu(x):
    return x * (0.5 * jnp.tanh(0.5 * x) + 0.5)


def _block(m):
    return np.block([[m.real, -m.imag], [m.imag, m.real]])


@functools.lru_cache(maxsize=None)
def _dft_tables(n):
    n1, n2 = FFT_N1, n // FFT_N1
    h1 = n1 // 2
    k1 = np.arange(n1)[:, None]
    ta_data, ta_filt = [], []
    for j in range(n2):
        m = np.exp(-2j * np.pi * (j * k1 / n + np.arange(n1)[None, :] * k1 / n1))
        ta_data.append(_block(m[:, :h1]))
        ta_filt.append(np.concatenate([m.real, m.imag], 0))
    f2 = np.exp(-2j * np.pi * np.arange(n2)[:, None] * np.arange(n2)[None, :] / n2)
    tb_fwd = _block(f2)
    t1 = np.arange(n2)[:, None]
    tb_inv = [_block(np.exp(2j * np.pi * (kk * t1 / n + np.arange(n2)[None, :] * t1 / n2)))
              for kk in range(n1)]
    hc = np.exp(2j * np.pi * np.arange(h1)[:, None] * np.arange(n1)[None, :] / n1) / n
    tc = _block(hc)
    f = lambda a: np.asarray(a, np.float32)
    pair = lambda t: np.concatenate([t[0::2], t[1::2]], axis=2)
    ta_data, tb_inv = np.stack(ta_data), np.stack(tb_inv)
    return dict(ta_filt=f(np.stack(ta_filt)), tb_fwd=f(tb_fwd), tc=f(tc),
                ta_pair=f(pair(ta_data)), tb_inv_pair=f(pair(tb_inv)),
                tb_fwd_pair=f(np.concatenate([tb_fwd, tb_fwd], axis=1)))


INPROJ_CHUNK = 1024


def _inproj_kernel(x_ref, w_ref, b_ref, cs_ref, o_ref, xb_ref, *, scaled):
    @pl.when(pl.program_id(1) == 0)
    def _():
        xb_ref[...] = x_ref[...].astype(BF16)

    tn = w_ref.shape[1]
    cw = min(tn, INPROJ_CHUNK)
    for c in range(tn // cw):
        sl = slice(c * cw, (c + 1) * cw)
        acc = jnp.dot(xb_ref[...], w_ref[:, sl], preferred_element_type=F32)
        if scaled:
            acc = acc * cs_ref[:, sl]
        val = (acc + b_ref[:, sl]).astype(o_ref.dtype)
        if len(o_ref.shape) == 3:
            n2, n1b, _ = o_ref.shape
            o_ref[:, :, sl] = _swap_major_sublane(val.reshape(n1b, n2, cw))
        else:
            o_ref[:, sl] = val


def _inproj(x, w, b, n_comp, col_scale=None, permute=None, tm=1024, tn=1024):
    m, k = x.shape
    n = w.shape[1]
    wc = n // n_comp
    tn = min(tn, wc)
    ncb = wc // tn
    scaled = col_scale is not None
    if not scaled:
        col_scale = b
    if permute is None:
        out_spec = pl.BlockSpec((None, tm, tn), lambda i, j: (j // ncb, i, j % ncb))
        out_shape = jax.ShapeDtypeStruct((n_comp, m, wc), BF16)
    else:
        l, n2 = permute
        tm = BF16_ROWS * n2
        tpb = l // tm
        out_spec = pl.BlockSpec((None, None, n2, BF16_ROWS, tn),
                                lambda i, j: (j // ncb, i // tpb, 0, i % tpb, j % ncb))
        out_shape = jax.ShapeDtypeStruct((n_comp, m // l, n2, l // n2, wc), BF16)
    vec = pl.BlockSpec((1, tn), lambda i, j: (0, j))
    w_mode = {"pipeline_mode": pl.Buffered(1)} if tn == n else {}
    return pl.pallas_call(
        functools.partial(_inproj_kernel, scaled=scaled),
        grid=(m // tm, n // tn),
        in_specs=[
            pl.BlockSpec((tm, k), lambda i, j: (i, 0)),
            pl.BlockSpec((k, tn), lambda i, j: (0, j), **w_mode),
            vec, vec,
        ],
        out_specs=out_spec,
        out_shape=out_shape,
        scratch_shapes=[pltpu.VMEM((tm, k), BF16)],
        compiler_params=_cparams("parallel", "arbitrary"),
        name="inproj",
    )(x, w, b, col_scale)


def _outproj_ln_kernel(y_ref, w_ref, b_ref, x_ref, g_ref, beta_ref, o_ref, *, alpha, permuted):
    y = y_ref[...]
    if permuted:
        nb, na, kk = y.shape
        y = y.reshape(nb * na, kk)
    h = jnp.dot(y, w_ref[...], preferred_element_type=F32) + b_ref[...]
    if permuted:
        h = _swap_major_sublane(h.reshape(nb, na, h.shape[-1]))
    r = alpha * x_ref[...] + h
    mu = jnp.mean(r, axis=-1, keepdims=True)
    d = r - mu
    var = jnp.mean(d * d, axis=-1, keepdims=True)
    o_ref[...] = d * lax.rsqrt(var + LN_EPS) * g_ref[...] + beta_ref[...]


def _outproj_ln(y, w, b, x, g, beta, alpha, tm=512):
    m, k = y.shape
    d = w.shape[1]
    row = lambda i: (i, 0)
    fixed = lambda i: (0, 0)
    return pl.pallas_call(
        functools.partial(_outproj_ln_kernel, alpha=alpha, permuted=False),
        grid=(m // tm,),
        in_specs=[
            pl.BlockSpec((tm, k), row),
            pl.BlockSpec((k, d), fixed),
            pl.BlockSpec((1, d), fixed),
            pl.BlockSpec((tm, d), row),
            pl.BlockSpec((1, d), fixed),
            pl.BlockSpec((1, d), fixed),
        ],
        out_specs=pl.BlockSpec((tm, d), row),
        out_shape=jax.ShapeDtypeStruct((m, d), F32),
        compiler_params=_cparams("parallel"),
        name="outproj_ln",
    )(y, w, b, x, g, beta)


def _outproj_ln_permuted(y, w, b, x, g, beta, alpha):
    bsz, n2, h1, k = y.shape
    d = w.shape[1]
    nb = F32_ROWS
    fixed = lambda bi, j: (0, 0)
    xspec = pl.BlockSpec((None, h1, nb, d), lambda bi, j: (bi, 0, j, 0))
    return pl.pallas_call(
        functools.partial(_outproj_ln_kernel, alpha=alpha, permuted=True),
        grid=(bsz, n2 // nb),
        in_specs=[
            pl.BlockSpec((None, nb, h1, k), lambda bi, j: (bi, j, 0, 0)),
            pl.BlockSpec((k, d), fixed),
            pl.BlockSpec((1, d), fixed),
            xspec,
            pl.BlockSpec((1, d), fixed),
            pl.BlockSpec((1, d), fixed),
        ],
        out_specs=xspec,
        out_shape=jax.ShapeDtypeStruct((bsz, h1, n2, d), F32),
        compiler_params=_cparams("parallel", "parallel"),
        name="outproj_ln_perm",
    )(y, w, b, x, g, beta)


def _filter_kernel(feat_ref, w1_ref, b1_ref, fr1_ref, w2_ref, b2_ref, fr2_ref, w3_ref, b3_ref,
                   fr3_ref, w4_ref, delta_ref, k_ref, ss_ref, h_ref):
    half = pl.program_id(0)
    hp = lax.Precision.HIGHEST

    @pl.when((pl.program_id(1) == 0) & (pl.program_id(2) == 0))
    def _():
        h = jnp.sin(fr1_ref[...] * (jnp.dot(feat_ref[...], w1_ref[...], precision=hp,
                                             preferred_element_type=F32) + b1_ref[...]))
        h = jnp.sin(fr2_ref[...] * (jnp.dot(h, w2_ref[...], precision=hp,
                                             preferred_element_type=F32) + b2_ref[...]))
        h = jnp.sin(fr3_ref[...] * (jnp.dot(h, w3_ref[...], precision=hp,
                                             preferred_element_type=F32) + b3_ref[...]))
        h_hi = h.astype(BF16)
        h_lo = (h - h_hi.astype(F32)).astype(BF16)
        h_ref[...] = jnp.concatenate([h_hi, h_hi, h_lo], axis=1)

    w4 = w4_ref[...]
    w_hi = w4.astype(BF16)
    w_lo = (w4 - w_hi.astype(F32)).astype(BF16)
    k = jnp.dot(h_ref[...], jnp.concatenate([w_hi, w_lo, w_hi], axis=0), preferred_element_type=F32)
    t_norm = feat_ref[:, 0:1]
    k = k * jnp.exp(-t_norm * delta_ref[...])
    row = lax.broadcasted_iota(jnp.int32, k.shape, 0)
    k = jnp.where((row == 0) & (half == 1), 0.0, k)
    ss_ref[...] = jnp.sum(k * k, axis=0, keepdims=True)
    n2, rows, ec = k_ref.shape
    k_ref[...] = _swap_major_sublane(k.astype(k_ref.dtype).reshape(rows, n2, ec))


def _filters(l, e, n2, w_f1, b_f1, fr1, w_f2, b_f2, fr2, w_f3, b_f3, fr3, w_f4, ec=512):
    fh = w_f1.shape[1]
    ec = min(ec, e)
    t_norm = jnp.linspace(0.0, 1.0, l, dtype=F32)
    w = 2.0 * math.pi * jnp.arange(l, dtype=F32) / l
    f = jnp.linspace(1e-4, POS_BANDS - 1, POS_BANDS, dtype=F32)
    ang = w[:, None] * f[None, :]
    feat = jnp.concatenate([t_norm[:, None], jnp.cos(ang), -jnp.sin(ang)], axis=-1)
    feat_rev = jnp.concatenate([feat[:1], feat[:0:-1]], axis=0)
    feat2 = jnp.pad(jnp.concatenate([feat, feat_rev], axis=0), ((0, 0), (0, LANE - POS_EMB_DIM)))
    w1p = jnp.pad(w_f1, ((0, LANE - POS_EMB_DIM), (0, 0)))
    deltas = jnp.abs(jnp.linspace(MIN_DECAY, MAX_DECAY, e, dtype=F32))[None, :]
    nec = e // ec
    rows = l // n2
    vec = lambda a: a.reshape(1, -1)
    fixed = lambda h, o, j: (0, 0)
    return pl.pallas_call(
        _filter_kernel,
        grid=(HY_DIRS, HY_ORDER, nec),
        in_specs=[
            pl.BlockSpec((l, LANE), lambda h, o, j: (h, 0)),
            pl.BlockSpec((LANE, fh), fixed), pl.BlockSpec((1, fh), fixed), pl.BlockSpec((1, fh), fixed),
            pl.BlockSpec((fh, fh), fixed), pl.BlockSpec((1, fh), fixed), pl.BlockSpec((1, fh), fixed),
            pl.BlockSpec((fh, fh), fixed), pl.BlockSpec((1, fh), fixed), pl.BlockSpec((1, fh), fixed),
            pl.BlockSpec((fh, ec), lambda h, o, j: (0, (o * HY_DIRS + h) * nec + j)),
            pl.BlockSpec((1, ec), lambda h, o, j: (0, j)),
        ],
        out_specs=[
            pl.BlockSpec((None, n2, rows, ec), lambda h, o, j: (o, 0, h, j)),
            pl.BlockSpec((None, None, 1, ec), lambda h, o, j: (o, h, 0, j)),
        ],
        out_shape=[
            jax.ShapeDtypeStruct((HY_ORDER, n2, HY_DIRS * rows, e), BF16),
            jax.ShapeDtypeStruct((HY_ORDER, HY_DIRS, 1, e), F32),
        ],
        scratch_shapes=[pltpu.VMEM((l, 3 * fh), BF16)],
        compiler_params=_cparams("arbitrary", "arbitrary", "arbitrary"),
        name="hyena_filter",
    )(feat2, w1p, vec(b_f1), vec(fr1), w_f2, vec(b_f2), vec(fr2), w_f3, vec(b_f3), vec(fr3),
      w_f4, deltas)


FFT_G = BF16_ROWS
FFT_EC = 512


def _fft_a_kernel(z_ref, ta_ref, are_ref, aim_ref):
    g = ta_ref.shape[0]
    n1 = are_ref.shape[0]
    rs = []
    for j in range(g):
        z = z_ref[:, j].reshape(ta_ref.shape[2], z_ref.shape[-1])
        rs.append(jnp.dot(ta_ref[j], z, preferred_element_type=F32).astype(BF16))
    rt = _swap_major_sublane(jnp.stack(rs))
    are_ref[...] = rt[:n1]
    aim_ref[...] = rt[n1:]


def _fft_a(z6, comp, ta, ec=FFT_EC, g=FFT_G):
    _, p, s, n2, r, e = z6.shape
    _, m2, kk = ta.shape
    n1 = m2 // 2
    ec = min(ec, e)
    out = jax.ShapeDtypeStruct((p, n1, n2, e), BF16)
    ospec = pl.BlockSpec((None, n1, g, ec), lambda pi, j, c: (pi, 0, j, c))
    return pl.pallas_call(
        _fft_a_kernel,
        grid=(p, n2 // g, e // ec),
        in_specs=[
            pl.BlockSpec((None, None, s, g, r, ec), lambda pi, j, c: (comp, pi, 0, j, 0, c)),
            pl.BlockSpec((g, m2, kk), lambda pi, j, c: (j, 0, 0)),
        ],
        out_specs=[ospec, ospec],
        out_shape=[out, out],
        compiler_params=_cparams("parallel", "arbitrary", "arbitrary"),
        name="fft_a",
    )(z6, ta)


def _fft_bf_kernel(are_ref, aim_ref, tf_ref, ss_ref, kre_ref, kim_ref):
    g, n2, _ = are_ref.shape
    rs = lax.rsqrt(ss_ref[0] + ss_ref[1] + 1e-12)
    for j in range(g):
        a = jnp.concatenate([are_ref[j], aim_ref[j]], axis=0)
        x = jnp.dot(tf_ref[...], a, preferred_element_type=F32) * rs
        kre_ref[j] = x[:n2].astype(kre_ref.dtype)
        kim_ref[j] = x[n2:].astype(kim_ref.dtype)


def _fft_b_filter(are, aim, tf, ss, ec=FFT_EC, g=FFT_G):
    p, n1, n2, e = are.shape
    ec = min(ec, e)
    spec = pl.BlockSpec((None, g, n2, ec), lambda j, c, pi: (pi, j, 0, c))
    out = jax.ShapeDtypeStruct((p, n1, n2, e), BF16)
    return pl.pallas_call(
        _fft_bf_kernel,
        grid=(n1 // g, e // ec, p),
        in_specs=[spec, spec,
                  pl.BlockSpec((2 * n2, 2 * n2), lambda j, c, pi: (0, 0)),
                  pl.BlockSpec((None, HY_DIRS, 1, ec), lambda j, c, pi: (pi, 0, 0, c))],
        out_specs=[spec, spec],
        out_shape=[out, out],
        compiler_params=_cparams("parallel", "arbitrary", "arbitrary"),
        name="fft_b_filter",
    )(are, aim, tf, ss)


HC_EC = LANE
HC_G = BF16_ROWS
STAGE_B_SPLIT = 1
HC_VMEM_LIMIT = 60 * 1024 * 1024


def _blockdiag(a, b):
    z = jnp.zeros_like(a)
    return jnp.concatenate([jnp.concatenate([a, z], axis=1),
                            jnp.concatenate([z, b], axis=1)], axis=0)


def _hyena_core_kernel(p_ref, wsc_ref, bsc_ref, hb_ref, kre_ref, kim_ref, tap_ref, tfp_ref,
                       tgp_ref, tc_ref, y_ref, a_scr, c_scr, z_scr):
    _, two, n2, h1, ec = p_ref.shape
    n1, g = a_scr.shape[2], a_scr.shape[3]
    rows = two * h1
    ngrp_a, ngrp_b = n2 // g, n1 // g
    half = g // 2
    rowi = lax.broadcasted_iota(jnp.int32, (rows, ec), 0) & (h1 - 1)

    def slab(comp, idx):
        return p_ref[comp, :, idx].reshape(rows, ec).astype(F32)

    def conv_group(comp, gi):
        base = gi * g
        first, last = gi == 0, gi == ngrp_a - 1
        w = [wsc_ref[t, comp:comp + 1, :].astype(BF16) for t in range(SHORT_CONV)]
        b = bsc_ref[comp:comp + 1, :].astype(BF16)
        raw = lambda idx: p_ref[comp, :, idx].reshape(rows, ec)
        prev = slab(comp, jnp.where(first, n2 - 1, base - 1))
        prev = jnp.where(first, jnp.where(rowi == 0, 0.0, pltpu.roll(prev, 1, axis=0)), prev)
        prev = prev.astype(BF16)
        cur = raw(base)
        for j in range(g):
            if j < g - 1:
                nxt = raw(base + j + 1)
            else:
                nxt = slab(comp, jnp.where(last, 0, base + g))
                nxt = jnp.where(last, jnp.where(rowi == h1 - 1, 0.0,
                                                pltpu.roll(nxt, rows - 1, axis=0)), nxt).astype(BF16)
            yield b + w[0] * prev + w[1] * cur + w[2] * nxt
            prev, cur = cur, nxt

    def stage_a_group(gi, zs):
        outs = []
        for jj in range(half):
            r = jnp.dot(tap_ref[gi * half + jj], _blockdiag(zs[2 * jj], zs[2 * jj + 1]),
                        preferred_element_type=F32).astype(BF16)
            outs += [r[:, :ec], r[:, ec:]]
        rt = _swap_major_sublane(jnp.stack(outs))
        a_scr[0, gi] = rt[:n1]
        a_scr[1, gi] = rt[n1:]

    def stage_b(order):
        def col(k1):
            return jnp.concatenate([a_scr[0, :, k1].reshape(n2, ec),
                                    a_scr[1, :, k1].reshape(n2, ec)], axis=0)

        def body(kb, carry):
            cs = []
            sub = half // STAGE_B_SPLIT
            for s0 in range(0, half, sub):
                pairs = range(s0, s0 + sub)
                xs = [jnp.dot(tfp_ref[...], _blockdiag(col(kb * g + 2 * jj), col(kb * g + 2 * jj + 1)),
                              preferred_element_type=F32) for jj in pairs]
                rhs = []
                for x, jj in zip(xs, pairs):
                    ka = kb * g + 2 * jj
                    kr = jnp.concatenate([kre_ref[order, ka], kre_ref[order, ka + 1]], axis=1)
                    ki = jnp.concatenate([kim_ref[order, ka], kim_ref[order, ka + 1]], axis=1)
                    xb = x.astype(BF16)
                    xr, xi = xb[:n2], xb[n2:]
                    yr = xr * kr - xi * ki
                    yi = xr * ki + xi * kr
                    rhs.append(_blockdiag(jnp.concatenate([yr[:, :ec], yi[:, :ec]], axis=0),
                                          jnp.concatenate([yr[:, ec:], yi[:, ec:]], axis=0)))
                for r, jj in zip(rhs, pairs):
                    c = jnp.dot(tgp_ref[kb * half + jj], r,
                                preferred_element_type=F32).astype(BF16)
                    cs += [c[:, :ec], c[:, ec:]]
            ct = _swap_major_sublane(jnp.stack(cs))
            c_scr[0, kb] = ct[:n2]
            c_scr[1, kb] = ct[n2:]
            return carry

        lax.fori_loop(0, ngrp_b, body, 0, unroll=True)

    def stage_c_slab(t1):
        c = jnp.concatenate([c_scr[0, :, t1].reshape(n1, ec), c_scr[1, :, t1].reshape(n1, ec)], axis=0)
        return jnp.dot(tc_ref[...], c, preferred_element_type=F32)

    def body_a1(gi, carry):
        zs = []
        for j, v in enumerate(conv_group(0, gi)):
            zs.append(v.astype(BF16))
            z_scr[gi * g + j] = zs[j]
        stage_a_group(gi, zs)
        return carry

    def body_c1(gi, carry):
        zs = []
        for j, gate in enumerate(conv_group(1, gi)):
            t1 = gi * g + j
            z1 = gate * (stage_c_slab(t1) + hb_ref[0:1, :] * z_scr[t1].astype(F32))
            zs.append(z1.astype(BF16))
            z_scr[t1] = zs[j]
        stage_a_group(gi, zs)
        return carry

    def body_c2(gi, carry):
        for j, gate in enumerate(conv_group(2, gi)):
            t1 = gi * g + j
            z2 = gate * (stage_c_slab(t1) + hb_ref[1:2, :] * z_scr[t1].astype(F32))
            out = z2 * _silu(slab(3, t1))
            y_ref[:, t1] = out.astype(BF16).reshape(two, h1, ec)
        return carry

    lax.fori_loop(0, ngrp_a, body_a1, 0, unroll=True)
    stage_b(0)
    lax.fori_loop(0, ngrp_a, body_c1, 0, unroll=True)
    stage_b(1)
    lax.fori_loop(0, ngrp_a, body_c2, 0, unroll=True)


def _hyena_core(p6, w_sc, b_sc, h_bias, kre, kim, tap, tfp, tgp, tc, ec=HC_EC, g=HC_G):
    _, p, two, n2, h1, e = p6.shape
    n1 = kre.shape[1]
    ec = min(ec, e)
    once = pl.Buffered(1)
    const = lambda shape: pl.BlockSpec(shape, lambda c, pi: (0,) * len(shape), pipeline_mode=once)
    kspec = pl.BlockSpec((HY_ORDER, n1, n2, ec), lambda c, pi: (0, 0, 0, c))
    return pl.pallas_call(
        _hyena_core_kernel,
        grid=(e // ec, p),
        in_specs=[
            pl.BlockSpec((4, None, two, n2, h1, ec), lambda c, pi: (0, pi, 0, 0, 0, c)),
            pl.BlockSpec((SHORT_CONV, 3, ec), lambda c, pi: (0, 0, c)),
            pl.BlockSpec((3, ec), lambda c, pi: (0, c)),
            pl.BlockSpec((HY_ORDER, ec), lambda c, pi: (0, c)),
            kspec, kspec,
            const(tap.shape), const(tfp.shape), const(tgp.shape), const(tc.shape),
        ],
        out_specs=pl.BlockSpec((None, two, n2, h1, ec), lambda c, pi: (pi, 0, 0, 0, c)),
        out_shape=jax.ShapeDtypeStruct((p, two, n2, h1, e), BF16),
        scratch_shapes=[pltpu.VMEM((2, n2 // g, n1, g, ec), BF16),
                        pltpu.VMEM((2, n1 // g, n2, g, ec), BF16),
                        pltpu.VMEM((n2, two * h1, ec), BF16)],
        compiler_params=_cparams("parallel", "arbitrary", vmem_limit=HC_VMEM_LIMIT),
        name="hyena_core",
    )(p6, w_sc.reshape(SHORT_CONV, 3, e), b_sc.reshape(3, e), h_bias, kre, kim, tap, tfp, tgp, tc)


ATT_QB = 8


def _attn_kernel(slope_ref, sink_ref, q_ref, *refs, group):
    nkb = ATT_QB + 2
    k_refs, v_refs = refs[:nkb], refs[nkb:2 * nkb]
    glo_ref, ghi_ref, o_ref = refs[2 * nkb:]
    for s in range(ATT_QB):
        rows = slice(s * WINDOW, (s + 1) * WINDOW)
        _attn_block(slope_ref, sink_ref, q_ref.at[rows], k_refs[s:s + 3], v_refs[s:s + 3],
                    glo_ref.at[rows], ghi_ref.at[rows], o_ref.at[rows],
                    pl.program_id(1) * ATT_QB + s, pl.num_programs(1) * ATT_QB, group)


def _attn_block(slope_ref, sink_ref, q_ref, k_refs, v_refs, glo_ref, ghi_ref, o_ref, n, nb, group):
    kp_ref, kc_ref, kn_ref = k_refs
    vp_ref, vc_ref, vn_ref = v_refs
    blk = q_ref.shape[0]
    hd = HEAD_DIM
    n_kv = kc_ref.shape[1] // hd
    heads_per_g = glo_ref.shape[1] // hd
    nk = 3 * blk
    sj = lax.broadcasted_iota(jnp.int32, (nk, blk), 0)
    qi = lax.broadcasted_iota(jnp.int32, (nk, blk), 1)
    dist = jnp.abs(qi - sj + blk)
    valid = (dist <= WINDOW) & ((sj >= blk) | (n > 0)) & ((sj < 2 * blk) | (n < nb - 1))
    distm = jnp.where(valid, dist.astype(F32), MASK_DIST)
    pad = BF16_ROWS
    first_q = lax.broadcasted_iota(jnp.int32, (pad, blk), 0) == 0
    first_d = lax.broadcasted_iota(jnp.int32, (pad, hd), 0) == 0
    ones_blk = jnp.ones((nk, hd), BF16)
    sink_v = jnp.concatenate([jnp.zeros((pad, hd), BF16),
                              jnp.where(first_d, 1.0, 0.0).astype(BF16)], axis=1)
    for kvh in range(n_kv):
        ksl = slice(kvh * hd, (kvh + 1) * hd)
        k = jnp.concatenate([kp_ref[:, ksl], kc_ref[:, ksl], kn_ref[:, ksl]], axis=0)
        v = jnp.concatenate([vp_ref[:, ksl], vc_ref[:, ksl], vn_ref[:, ksl]], axis=0)
        v_aug = jnp.concatenate([jnp.concatenate([v, ones_blk], axis=1), sink_v], axis=0)
        heads = range(kvh * group, (kvh + 1) * group)
        q = jnp.concatenate([q_ref[:, h * hd:(h + 1) * hd] for h in heads], axis=0)
        st = lax.dot_general(k, q, (((1,), (1,)), ((), ())), preferred_element_type=F32)
        pts = []
        for gi, h in enumerate(heads):
            logit = st[:, gi * blk:(gi + 1) * blk] - slope_ref[h] * distm
            sk = sink_ref[h]
            m = jnp.maximum(jnp.max(logit, axis=0, keepdims=True), sk)
            p = jnp.exp2(logit - m)
            sink_p = jnp.where(first_q, jnp.exp2(sk - m), 0.0)
            pts.append(jnp.concatenate([p, sink_p], axis=0).astype(BF16))
        oa = lax.dot_general(jnp.concatenate(pts, axis=1), v_aug, (((0,), (0,)), ((), ())),
                             preferred_element_type=F32)
        for gi, h in enumerate(heads):
            rows = slice(gi * blk, (gi + 1) * blk)
            o = oa[rows, :hd] * (1.0 / oa[rows, hd:])
            g_half = glo_ref if h < heads_per_g else ghi_ref
            hg = h % heads_per_g
            gate = _silu(g_half[:, hg * hd:(hg + 1) * hd].astype(F32))
            o_ref[:, h * hd:(h + 1) * hd] = (o * gate).astype(o_ref.dtype)


def _attention(proj, slopes, sink, n_heads):
    bsz, l, _ = proj.shape
    hd, blk = HEAD_DIM, WINDOW
    group = n_heads // N_KV_HEADS
    aw = n_heads * hd
    kvw = N_KV_HEADS * hd
    nb = l // blk
    k0 = aw // kvw
    v0 = k0 + 1
    gw = aw // 2
    g0 = (aw + 2 * kvw) // gw
    smem = pl.BlockSpec(memory_space=pltpu.SMEM)
    qb = ATT_QB
    kv = lambda c0, off: pl.BlockSpec(
        (None, blk, kvw), lambda b, n: (b, jnp.clip(qb * n + off, 0, nb - 1), c0))
    kvs = lambda c0: [kv(c0, off) for off in range(-1, qb + 1)]
    gspec = lambda c0: pl.BlockSpec((None, qb * blk, gw), lambda b, n: (b, n, c0))
    assert (aw + 2 * kvw) % gw == 0 and aw % kvw == 0 and nb % qb == 0
    n_in = 3 + 2 * (qb + 2) + 2
    return pl.pallas_call(
        functools.partial(_attn_kernel, group=group),
        grid=(bsz, nb // qb),
        in_specs=[smem, smem,
                  pl.BlockSpec((None, qb * blk, aw), lambda b, n: (b, n, 0)),
                  *kvs(k0), *kvs(v0), gspec(g0), gspec(g0 + 1)],
        out_specs=pl.BlockSpec((None, qb * blk, aw), lambda b, n: (b, n, 0)),
        out_shape=jax.ShapeDtypeStruct((bsz, l, aw), BF16),
        compiler_params=_cparams("parallel", "arbitrary"),
        name="swa_attention",
    )(slopes, sink, *([proj] * (n_in - 2)))


def _hyena_filter_spectrum(l, e, hy, tabs):
    n1 = FFT_N1
    n2 = 2 * l // n1
    k_un, ss = _filters(l, e, n2, hy["w_f1"], hy["b_f1"], hy["fr1"], hy["w_f2"], hy["b_f2"],
                        hy["fr2"], hy["w_f3"], hy["b_f3"], hy["fr3"], hy["w_f4"])
    k6 = k_un.reshape(1, HY_ORDER, 1, n2, n1, e)
    are, aim = _fft_a(k6, 0, tabs["ta_filt"], ec=2 * FFT_EC)
    return _fft_b_filter(are, aim, tabs["tb_fwd"], ss, ec=4 * FFT_EC)


def _hyena_layer(x, hy, kf, tabs, ln_g, ln_b, alpha):
    bsz, l, d = x.shape
    e = hy["w_out"].shape[0]
    kre, kim = kf
    n1 = FFT_N1
    n2 = 2 * l // n1
    h1 = n1 // 2
    proj = _inproj(x.reshape(bsz * l, d), hy["w_in"], hy["b_in"], 4, permute=(l, n2),
                   tn=2 * INPROJ_CHUNK)
    y = _hyena_core(proj.reshape(4, bsz // 2, 2, n2, h1, e), hy["w_sc"], hy["b_sc"], hy["h_bias"],
                    kre, kim, tabs["ta_pair"], tabs["tb_fwd_pair"], tabs["tb_inv_pair"],
                    tabs["tc"])
    out = _outproj_ln_permuted(y.reshape(bsz, n2, h1, e), hy["w_out"], hy["b_out"],
                               x.reshape(bsz, h1, n2, d), ln_g, ln_b, alpha)
    return out.reshape(bsz, l, d)


def _attention_layer(x, at, ln_g, ln_b, alpha):
    bsz, l, d = x.shape
    n_heads = at["sink"].shape[0]
    x2 = x.reshape(bsz * l, d)
    proj = _inproj(x2, at["w_in"], at["b_in"], 1, col_scale=at["col_scale"],
                   tm=512, tn=at["w_in"].shape[1])
    proj = proj.reshape(bsz, l, -1)
    o = _attention(proj, at["slopes"], at["sink"] * LOG2E, n_heads)
    out = _outproj_ln(o.reshape(bsz * l, -1), at["w_out"], at["b_out"], x2, ln_g, ln_b, alpha)
    return out.reshape(bsz, l, d)


def kernel(x_prompt, x_sample, ln_g, ln_b, hy_w_in, hy_b_in, hy_w_sc, hy_b_sc, hy_w_f1, hy_b_f1,
           hy_fr1, hy_w_f2, hy_b_f2, hy_fr2, hy_w_f3, hy_b_f3, hy_fr3, hy_w_f4, hy_h_bias,
           hy_w_out, hy_b_out, at_w_in, at_sink, at_w_out):
    depth = ln_g.shape[0]
    alpha = (2 * depth) ** 0.25
    l = x_prompt.shape[1]
    d = x_prompt.shape[2]
    assert x_sample.shape[1] == l and (2 * l) % FFT_N1 == 0
    assert x_prompt.shape[0] % 2 == 0 and x_sample.shape[0] % 2 == 0
    tabs = {k: jnp.asarray(t).astype(BF16) for k, t in _dft_tables(2 * l).items()}
    row = lambda a: a.reshape(1, -1)

    layers = []
    for i in range(depth):
        j = i // N_MIXERS
        if i % N_MIXERS == 0:
            e = hy_w_out.shape[1]
            hy = dict(w_in=hy_w_in[j].astype(BF16), b_in=row(hy_b_in[j]), w_sc=hy_w_sc[j],
                      b_sc=row(hy_b_sc[j]), w_f1=hy_w_f1[j], b_f1=hy_b_f1[j], fr1=hy_fr1[j],
                      w_f2=hy_w_f2[j], b_f2=hy_b_f2[j], fr2=hy_fr2[j], w_f3=hy_w_f3[j],
                      b_f3=hy_b_f3[j], fr3=hy_fr3[j], w_f4=hy_w_f4[j], h_bias=hy_h_bias[j],
                      w_out=hy_w_out[j].astype(BF16), b_out=row(hy_b_out[j]))
            kf = _hyena_filter_spectrum(l, e, hy, tabs)
            layers.append(("hyena", hy, kf))
        else:
            n_heads = at_sink.shape[1]
            aw = n_heads * HEAD_DIM
            width = at_w_in.shape[2]
            col_scale = jnp.where(jnp.arange(width) < aw, HEAD_DIM ** -0.5 * LOG2E, 1.0)
            slopes = jnp.exp2(-8.0 * jnp.arange(1, n_heads + 1, dtype=F32) / n_heads) * LOG2E
            at = dict(w_in=at_w_in[j].astype(BF16), b_in=jnp.zeros((1, width), F32),
                      col_scale=row(col_scale.astype(F32)), slopes=slopes,
                      sink=at_sink[j], w_out=at_w_out[j].astype(BF16),
                      b_out=jnp.zeros((1, d), F32))
            layers.append(("attn", at, None))

    def trunk(x):
        for i, (kind, prm, kf) in enumerate(layers):
            g, b = row(ln_g[i]), row(ln_b[i])
            if kind == "hyena":
                x = _hyena_layer(x, prm, kf, tabs, g, b, alpha)
            else:
                x = _attention_layer(x, prm, g, b, alpha)
        return x

    return (trunk(x_prompt), trunk(x_sample))
```
